```python
import math
import jax, jax.numpy as jnp
from jax import lax
import numpy as np

D_MODEL = 1024
BATCH = 2
SEQ = 8192
DEPTH = 2

CTX_LEN = 256
GRID_W = 64
N_GROUPS = 4
N_HEADS_GROUP = 4
HEAD_DIM = D_MODEL // (N_GROUPS * N_HEADS_GROUP)
GROUP_WIDTH = N_HEADS_GROUP * HEAD_DIM
MIX_WIDTH = N_GROUPS * GROUP_WIDTH
ROPE_DIM = HEAD_DIM // 2
DIFF_QK_DIM = HEAD_DIM // 2
MLA_Q_LORA = D_MODEL // 4
MLA_KV_LORA = D_MODEL // 8
MLA_NOPE = HEAD_DIM
MLA_ROPE = ROPE_DIM
MLA_V = HEAD_DIM
MLSTM_CHUNK = 64
MLSTM_CONV = 3
NA_WIN_ROWS = 8
NA_WIN_COLS = 16
FFN_HIDDEN = 2816
FFN_CONV = 3
ROPE_BASE = 10000.0
NORM_EPS = 1e-6
ATTN_BLOCK = 128
IN_LAYOUT = (
    ('diff_q', N_HEADS_GROUP * 2 * DIFF_QK_DIM),
    ('diff_k', N_HEADS_GROUP * 2 * DIFF_QK_DIM),
    ('diff_v', N_HEADS_GROUP * HEAD_DIM),
    ('mla_cq', MLA_Q_LORA),
    ('mla_ckv', MLA_KV_LORA),
    ('mla_kr', MLA_ROPE),
    ('ml_q', GROUP_WIDTH),
    ('ml_k', GROUP_WIDTH),
    ('ml_v', GROUP_WIDTH),
    ('ml_o', GROUP_WIDTH),
    ('ml_i', 2 * N_HEADS_GROUP),
    ('ml_f', 2 * N_HEADS_GROUP),
    ('na_q', GROUP_WIDTH),
    ('na_k', GROUP_WIDTH),
    ('na_v', GROUP_WIDTH),
)
IN_WIDTH = sum(n for _, n in IN_LAYOUT)

kernel_name = 'hybrid_headgroup_diffusion_trunk'


def rmsnorm(x, g):
    xf = x.astype(jnp.float32)
    y = xf * lax.rsqrt(jnp.mean(xf * xf, axis=-1, keepdims=True) + NORM_EPS)
    return (y * g.astype(jnp.float32)).astype(x.dtype)


def split_cols(p):
    out, off = {}, 0
    for name, n in IN_LAYOUT:
        out[name] = p[..., off:off + n]
        off += n
    return out


def dwconv(x, w):
    k = w.shape[0]
    return lax.conv_general_dilated(x, w[:, None, :].astype(x.dtype), (1,), [(k // 2, k // 2)],
                                    dimension_numbers=('NWC', 'WIO', 'NWC'),
                                    feature_group_count=x.shape[-1])


def rope_tables(t_len):
    nf = ROPE_DIM // 4
    t = jnp.arange(t_len, dtype=jnp.int32)
    row = (t // GRID_W).astype(jnp.float32)
    col = (t % GRID_W).astype(jnp.float32)
    inv = ROPE_BASE ** (-jnp.arange(nf, dtype=jnp.float32) / nf)
    ar, ac = row[:, None] * inv, col[:, None] * inv
    return jnp.cos(ar), jnp.sin(ar), jnp.cos(ac), jnp.sin(ac)


def apply_axial_rope(t, rope):
    cr, sr, cc, sc = rope
    shp = (1, t.shape[1]) + (1,) * (t.ndim - 3) + (cr.shape[-1],)

    def rot(u, cos, sin):
        cos = cos.reshape(shp).astype(u.dtype)
        sin = sin.reshape(shp).astype(u.dtype)
        u1, u2 = jnp.split(u, 2, axis=-1)
        return jnp.concatenate([u1 * cos - u2 * sin, u1 * sin + u2 * cos], axis=-1)

    tr, tc = jnp.split(t, 2, axis=-1)
    return jnp.concatenate([rot(tr, cr, sr), rot(tc, cc, sc)], axis=-1)


def block_attention(q, k, v, map_w, scale):
    b, s_len, g, m, dk = q.shape
    blk = min(ATTN_BLOCK, s_len)
    qb = jnp.moveaxis(q.reshape(b, s_len // blk, blk, g, m, dk), 1, 0)
    w = map_w.astype(jnp.float32)

    def one(qi):
        s = jnp.einsum('bqgmd,bkgmd->bgmqk', qi, k, preferred_element_type=jnp.float32) * scale
        p = jnp.einsum('bgmqk,m->bgqk', jax.nn.softmax(s, axis=-1), w)
        return jnp.einsum('bgqk,bkgd->bqgd', p.astype(v.dtype), v)

    o = lax.map(one, qb)
    return jnp.moveaxis(o, 0, 1).reshape(b, s_len, g, v.shape[-1])


def diff_attention_mixer(cols, cols_c, g_q, g_k, lam_vec, g_out, lam_init, rope, need_ctx):
    H, dk = N_HEADS_GROUP, DIFF_QK_DIM

    def qk(t, g):
        return rmsnorm(t.reshape(t.shape[0], t.shape[1], H, 2, dk), g)

    def vals(t):
        return t.reshape(t.shape[0], t.shape[1], H, HEAD_DIM)

    q = apply_axial_rope(qk(cols['diff_q'], g_q), rope)
    k = apply_axial_rope(qk(cols['diff_k'], g_k), rope)
    kc, vc = qk(cols_c['diff_k'], g_k), vals(cols_c['diff_v'])
    lv = lam_vec.astype(jnp.float32)
    lam = jnp.exp(jnp.dot(lv[0], lv[1])) - jnp.exp(jnp.dot(lv[2], lv[3])) + lam_init
    map_w = jnp.stack([jnp.ones_like(lam), -lam])
    scale = dk ** -0.5

    def finish(o):
        o = rmsnorm(o, g_out) * (1.0 - lam_init)
        return o.reshape(o.shape[0], o.shape[1], H * HEAD_DIM)

    y = finish(block_attention(q, jnp.concatenate([k, kc], 1),
                               jnp.concatenate([vals(cols['diff_v']), vc], 1), map_w, scale))
    yc = finish(block_attention(qk(cols_c['diff_q'], g_q), kc, vc, map_w, scale)) if need_ctx else None
    return y, yc


def mla_mixer(cols, cols_c, g_cq, g_ckv, w_uq, w_ukv, g_q, g_k, rope, need_ctx):
    H = N_HEADS_GROUP

    def queries(c):
        q = rmsnorm(c['mla_cq'], g_cq) @ w_uq
        return rmsnorm(q.reshape(q.shape[0], q.shape[1], H, MLA_NOPE + MLA_ROPE), g_q)

    def keys_values(c):
        kv = rmsnorm(c['mla_ckv'], g_ckv) @ w_ukv
        kv = kv.reshape(kv.shape[0], kv.shape[1], H, MLA_NOPE + MLA_V)
        k_nope, v = kv[..., :MLA_NOPE], kv[..., MLA_NOPE:]
        k_rope = jnp.broadcast_to(c['mla_kr'][:, :, None, :], k_nope.shape[:3] + (MLA_ROPE,))
        return rmsnorm(jnp.concatenate([k_nope, k_rope], -1), g_k), v

    def rope_part(t):
        return jnp.concatenate([t[..., :MLA_NOPE], apply_axial_rope(t[..., MLA_NOPE:], rope)], -1)

    q = rope_part(queries(cols))
    k, v = keys_values(cols)
    k = rope_part(k)
    kc, vc = keys_values(cols_c)
    w1 = jnp.ones((1,), jnp.float32)
    scale = (MLA_NOPE + MLA_ROPE) ** -0.5
    o = block_attention(q[:, :, :, None], jnp.concatenate([k, kc], 1)[:, :, :, None],
                        jnp.concatenate([v, vc], 1), w1, scale)
    y = o.reshape(o.shape[0], o.shape[1], H * MLA_V)
    yc = None
    if need_ctx:
        oc = block_attention(queries(cols_c)[:, :, :, None], kc[:, :, :, None], vc, w1, scale)
        yc = oc.reshape(oc.shape[0], oc.shape[1], H * MLA_V)
    return y, yc


def mlstm_chunkwise(q, k, v, ig, lf, state0, want_out):
    b_, t_len, H, d = q.shape
    L = min(MLSTM_CHUNK, t_len)
    N = t_len // L

    def chunks(t):
        return jnp.moveaxis(t.reshape((b_, N, L) + t.shape[2:]), 3, 1)

    qc, kc, vc, igc, lfc = (chunks(t) for t in (q, k, v, ig, lf))
    b = jnp.cumsum(lfc, axis=-1)
    g = b[..., -1]
    a = g[..., None] - b + igc
    m_loc = jnp.max(a, axis=-1)
    w = jnp.exp(a - m_loc[..., None])
    c_loc = jnp.einsum('bhnl,bhnld,bhnle->bhnde', w, vc, kc)
    n_loc = jnp.einsum('bhnl,bhnle->bhne', w, kc)

    def step(carry, inp):
        c_s, n_s, m_s = carry
        g_j, ml_j, cl_j, nl_j = inp
        m_new = jnp.maximum(g_j + m_s, ml_j)
        a_old = jnp.exp(g_j + m_s - m_new)
        a_loc = jnp.exp(ml_j - m_new)
        c_new = a_old[..., None, None] * c_s + a_loc[..., None, None] * cl_j
        n_new = a_old[..., None] * n_s + a_loc[..., None] * nl_j
        return (c_new, n_new, m_new), (c_s, n_s, m_s)

    xs = tuple(jnp.moveaxis(t, 2, 0) for t in (g, m_loc, c_loc, n_loc))
    final, prev = lax.scan(step, state0, xs)
    if not want_out:
        return None, final
    c_prev, n_prev, m_prev = (jnp.moveaxis(t, 0, 2) for t in prev)
    tri = jnp.tril(jnp.ones((L, L), dtype=bool))
    dlog = jnp.where(tri, b[..., :, None] - b[..., None, :] + igc[..., None, :], -jnp.inf)
    inter = b + m_prev[..., None]
    m_t = jnp.maximum(jnp.max(dlog, axis=-1), inter)
    s = jnp.einsum('bhnld,bhnsd->bhnls', qc, kc) * jnp.exp(dlog - m_t[..., None])
    e = jnp.exp(inter - m_t)
    num = jnp.einsum('bhnls,bhnsd->bhnld', s, vc) + e[..., None] * jnp.einsum('bhnde,bhnle->bhnld', c_prev, qc)
    den = jnp.sum(s, axis=-1) + e * jnp.einsum('bhne,bhnle->bhnl', n_prev, qc)
    h = num / jnp.maximum(jnp.abs(den), jnp.exp(-m_t))[..., None]
    return jnp.moveaxis(h, 1, 3).reshape(b_, t_len, H, d), final


def mlstm_mixer(cols, cols_c, w_conv, b_i, b_f, g_out, need_ctx):
    H, d = N_HEADS_GROUP, HEAD_DIM
    f32 = jnp.float32

    def prep(c):
        bb, t_len = c['ml_q'].shape[:2]
        qk = jax.nn.silu(dwconv(jnp.concatenate([c['ml_q'], c['ml_k']], -1), w_conv))
        q, k = jnp.split(qk, 2, axis=-1)
        q = q.reshape(bb, t_len, H, d).astype(f32)
        k = k.reshape(bb, t_len, H, d).astype(f32) * (d ** -0.5)
        v = c['ml_v'].reshape(bb, t_len, H, d).astype(f32)
        ig = c['ml_i'].reshape(bb, t_len, 2, H).astype(f32) + b_i.astype(f32)
        lf = jax.nn.log_sigmoid(c['ml_f'].reshape(bb, t_len, 2, H).astype(f32) + b_f.astype(f32))
        return q, k, v, ig, lf

    lat, ctxp = prep(cols), prep(cols_c)
    bb = lat[0].shape[0]
    zero_state = (jnp.zeros((bb, H, d, d), f32), jnp.zeros((bb, H, d), f32), jnp.zeros((bb, H), f32))
    h_lat, h_ctx = [], []
    for direction in range(2):
        def order(t):
            return t[:, ::-1] if direction == 1 else t

        def seq(p):
            q, k, v, ig, lf = p
            return order(q), order(k), order(v), order(ig[:, :, direction]), order(lf[:, :, direction])

        hc, state_c = mlstm_chunkwise(*seq(ctxp), zero_state, need_ctx)
        hl, _ = mlstm_chunkwise(*seq(lat), state_c, True)
        h_lat.append(order(hl))
        if need_ctx:
            h_ctx.append(order(hc))

    def finish(h, c):
        o = jax.nn.sigmoid(c['ml_o'].astype(f32)).reshape(h.shape)
        y = (rmsnorm(h, g_out) * o).astype(c['ml_o'].dtype)
        return y.reshape(h.shape[0], h.shape[1], H * d)

    y = finish(h_lat[0] + h_lat[1], cols)
    yc = finish(h_ctx[0] + h_ctx[1], cols_c) if need_ctx else None
    return y, yc


def neighbourhood_mixer(cols, cols_c, g_q, g_k, rpb, need_ctx):
    H, d = N_HEADS_GROUP, HEAD_DIM

    def heads(t):
        return t.reshape(t.shape[0], t.shape[1], H, d)

    q = rmsnorm(heads(cols['na_q']), g_q)
    k = rmsnorm(heads(cols['na_k']), g_k)
    v = heads(cols['na_v'])
    kc = rmsnorm(heads(cols_c['na_k']), g_k)
    vc = heads(cols_c['na_v'])
    bb, s_len = q.shape[:2]
    W = GRID_W
    R = s_len // W
    wr = min(NA_WIN_ROWS, R)
    scale = d ** -0.5
    qg = q.reshape(bb, R, W, H, d)
    r = jnp.arange(R)
    ridx = jnp.clip(r - wr // 2, 0, R - wr)[:, None] + jnp.arange(wr)[None, :]
    kw = k.reshape(bb, R, W, H, d)[:, ridx]
    vw = v.reshape(bb, R, W, H, d)[:, ridx]
    cidx = jnp.arange(W)
    cs = jnp.clip(cidx - NA_WIN_COLS // 2, 0, W - NA_WIN_COLS)
    col_ok = (cidx[None, :] >= cs[:, None]) & (cidx[None, :] < cs[:, None] + NA_WIN_COLS)
    roff = ridx - r[:, None] + (NA_WIN_ROWS - 1)
    coff = jnp.clip(cidx[None, :] - cidx[:, None], 1 - NA_WIN_COLS, NA_WIN_COLS - 1) + (NA_WIN_COLS - 1)
    bias = rpb[:, roff[:, None, :, None], coff[None, :, None, :]]
    s_win = jnp.einsum('brqhd,brjkhd->bhrqjk', qg, kw, preferred_element_type=jnp.float32) * scale
    s_win = jnp.where(col_ok[:, None, :], s_win + bias[None].astype(jnp.float32), -jnp.inf)
    s_ctx = jnp.einsum('brqhd,bchd->bhrqc', qg, kc, preferred_element_type=jnp.float32) * scale
    p = jax.nn.softmax(jnp.concatenate([s_win.reshape(bb, H, R, W, wr * W), s_ctx], -1), axis=-1)
    p_win = p[..., :wr * W].reshape(bb, H, R, W, wr, W).astype(v.dtype)
    p_ctx = p[..., wr * W:].astype(v.dtype)
    o = jnp.einsum('bhrqjk,brjkhd->brqhd', p_win, vw) + jnp.einsum('bhrqc,bchd->brqhd', p_ctx, vc)
    y = o.reshape(bb, s_len, H * d)
    yc = None
    if need_ctx:
        qc = rmsnorm(heads(cols_c['na_q']), g_q)
        oc = block_attention(qc[:, :, :, None], kc[:, :, :, None], vc, jnp.ones((1,), jnp.float32), scale)
        yc = oc.reshape(oc.shape[0], oc.shape[1], H * d)
    return y, yc


def conv_ffn(h, w_up, w_conv, w_down):
    u = dwconv(h @ w_up, w_conv)
    a, g = jnp.split(u, 2, axis=-1)
    return (jax.nn.silu(g) * a) @ w_down


def setup_inputs(seed: int = 0) -> dict:
    key = jax.random.key(seed)
    ks = iter(jax.random.split(key, 40))
    L, D, H = DEPTH, D_MODEL, N_HEADS_GROUP

    def nrm(shape, s):
        return s * jax.random.normal(next(ks), shape, jnp.float32)

    def gain(shape):
        return 1.0 + nrm(shape, 0.1)

    return {
        'x': nrm((BATCH, SEQ, D), 1.0),
        'c': nrm((BATCH, D), 1.0),
        'ctx': nrm((BATCH, CTX_LEN, D), 1.0),
        'c_ctx': nrm((D,), 1.0),
        'w_mod': nrm((L, D, 6 * D), 0.5 * D ** -0.5),
        'b_mod': nrm((L, 6 * D), 0.02),
        'g_norm1': gain((L, D)),
        'g_norm2': gain((L, D)),
        'w_in': nrm((L, D, IN_WIDTH), D ** -0.5),
        'w_out': nrm((L, MIX_WIDTH, D), MIX_WIDTH ** -0.5),
        'diff_g_q': gain((L, DIFF_QK_DIM)),
        'diff_g_k': gain((L, DIFF_QK_DIM)),
        'diff_lam': nrm((L, 4, DIFF_QK_DIM), 0.1),
        'diff_g_out': gain((L, HEAD_DIM)),
        'mla_g_cq': gain((L, MLA_Q_LORA)),
        'mla_g_ckv': gain((L, MLA_KV_LORA)),
        'mla_w_uq': nrm((L, MLA_Q_LORA, H * (MLA_NOPE + MLA_ROPE)), MLA_Q_LORA ** -0.5),
        'mla_w_ukv': nrm((L, MLA_KV_LORA, H * (MLA_NOPE + MLA_V)), MLA_KV_LORA ** -0.5),
        'mla_g_q': gain((L, MLA_NOPE + MLA_ROPE)),
        'mla_g_k': gain((L, MLA_NOPE + MLA_ROPE)),
        'mlstm_w_conv': nrm((L, MLSTM_CONV, 2 * GROUP_WIDTH), MLSTM_CONV ** -0.5),
        'mlstm_b_i': nrm((L, 2, H), 0.1),
        'mlstm_b_f': jnp.linspace(3.0, 6.0, H, dtype=jnp.float32) + nrm((L, 2, H), 0.1),
        'mlstm_g_out': gain((L, HEAD_DIM)),
        'na_g_q': gain((L, HEAD_DIM)),
        'na_g_k': gain((L, HEAD_DIM)),
        'na_rpb': nrm((L, H, 2 * NA_WIN_ROWS - 1, 2 * NA_WIN_COLS - 1), 0.1),
        'ffn_w_up': nrm((L, D, 2 * FFN_HIDDEN), D ** -0.5),
        'ffn_w_conv': nrm((L, FFN_CONV, 2 * FFN_HIDDEN), FFN_CONV ** -0.5),
        'ffn_w_down': nrm((L, FFN_HIDDEN, D), FFN_HIDDEN ** -0.5),
    }


def reference(x, c, ctx, c_ctx, w_mod, b_mod, g_norm1, g_norm2, w_in, w_out,
              diff_g_q, diff_g_k, diff_lam, diff_g_out,
              mla_g_cq, mla_g_ckv, mla_w_uq, mla_w_ukv, mla_g_q, mla_g_k,
              mlstm_w_conv, mlstm_b_i, mlstm_b_f, mlstm_g_out,
              na_g_q, na_g_k, na_rpb, ffn_w_up, ffn_w_conv, ffn_w_down):
    rope = rope_tables(x.shape[1])
    xc = ctx
    for l in range(DEPTH):
        need_ctx = l < DEPTH - 1
        lam_init = 0.8 - 0.6 * math.exp(-0.3 * l)
        mod = jax.nn.silu(c) @ w_mod[l] + b_mod[l]
        mod_c = jax.nn.silu(c_ctx) @ w_mod[l] + b_mod[l]
        sh1, sc1, gt1, sh2, sc2, gt2 = jnp.split(mod[:, None, :], 6, axis=-1)
        csh1, csc1, cgt1, csh2, csc2, cgt2 = jnp.split(mod_c, 6, axis=-1)
        h = rmsnorm(x, g_norm1[l]) * (1.0 + sc1) + sh1
        hc = rmsnorm(xc, g_norm1[l]) * (1.0 + csc1) + csh1
        cols = split_cols(h @ w_in[l])
        cols_c = split_cols(hc @ w_in[l])
        y_diff, cy_diff = diff_attention_mixer(cols, cols_c, diff_g_q[l], diff_g_k[l], diff_lam[l],
                                               diff_g_out[l], lam_init, rope, need_ctx)
        y_mla, cy_mla = mla_mixer(cols, cols_c, mla_g_cq[l], mla_g_ckv[l], mla_w_uq[l], mla_w_ukv[l],
                                  mla_g_q[l], mla_g_k[l], rope, need_ctx)
        y_ml, cy_ml = mlstm_mixer(cols, cols_c, mlstm_w_conv[l], mlstm_b_i[l], mlstm_b_f[l],
                                  mlstm_g_out[l], need_ctx)
        y_na, cy_na = neighbourhood_mixer(cols, cols_c, na_g_q[l], na_g_k[l], na_rpb[l], need_ctx)
        x = x + gt1 * (jnp.concatenate([y_diff, y_mla, y_ml, y_na], axis=-1) @ w_out[l])
        h2 = rmsnorm(x, g_norm2[l]) * (1.0 + sc2) + sh2
        x = x + gt2 * conv_ffn(h2, ffn_w_up[l], ffn_w_conv[l], ffn_w_down[l])
        if need_ctx:
            xc = xc + cgt1 * (jnp.concatenate([cy_diff, cy_mla, cy_ml, cy_na], axis=-1) @ w_out[l])
            hc2 = rmsnorm(xc, g_norm2[l]) * (1.0 + csc2) + csh2
            xc = xc + cgt2 * conv_ffn(hc2, ffn_w_up[l], ffn_w_conv[l], ffn_w_down[l])
    return x
```

```python
import functools
import math

import numpy as np
import jax
import jax.numpy as jnp
from jax import lax
from jax.experimental import pallas as pl
from jax.experimental.pallas import tpu as pltpu

F32 = jnp.float32
BF16 = jnp.bfloat16
HIGHEST = lax.Precision.HIGHEST

D_MODEL = 1024
GRID_W = 64
N_HEADS = 4
HEAD_DIM = 64
GROUP_W = N_HEADS * HEAD_DIM
DIFF_DK = 32
MLA_Q_LORA = 256
MLA_KV_LORA = 128
MLA_NOPE = 64
MLA_ROPE = 32
MLA_DK = MLA_NOPE + MLA_ROPE
MLA_PAD = 128
MLSTM_CHUNK = 64
NA_ROWS = 8
NA_COLS = 16
FFN_HIDDEN = 2816
ROPE_BASE = 10000.0
NORM_EPS = 1e-6
LOG2E = 1.4426950408889634
NEG = -1e30

ROW_TILE = 256
FFN_CHUNK = 256
FFN_HALO = 16
VMEM_LIMIT = 56 * 1024 * 1024

C_DQ, C_DK, C_DV, C_CQ, C_CKV, C_KR = 0, 256, 512, 768, 1024, 1152
C_LQ, C_LV, C_LO, C_LG, C_NQ, C_NK, C_NV = 1280, 1792, 2048, 2304, 2432, 2688, 2944
W1_COLS = 3200
V_DGQ, V_DGK, V_GCQ, V_GCKV, V_MGQ, V_MGK, V_NGQ, V_NGK, V_GB, V_CW = 0, 1, 2, 3, 4, 5, 6, 7, 8, 9


def _cparams(sem):
    return pltpu.CompilerParams(dimension_semantics=sem, vmem_limit_bytes=VMEM_LIMIT)


def _dot(a, b):
    return jnp.dot(a, b, preferred_element_type=F32)


def _dot_nt(a, b):
    return lax.dot_general(a, b, (((1,), (1,)), ((), ())), preferred_element_type=F32)


def _dot_tn(a, b):
    return lax.dot_general(a, b, (((0,), (0,)), ((), ())), preferred_element_type=F32)


def _dot_hilo(x, w):
    hi = x.astype(BF16)
    lo = (x - hi.astype(F32)).astype(BF16)
    return _dot(hi, w) + _dot(lo, w)


def _sigmoid(x):
    return 1.0 / (1.0 + jnp.exp(-x))


def _group_sumsq(x, e_ref):
    w = x.shape[1]
    sq = (x * x).astype(BF16)
    return jnp.concatenate([_dot(sq[:, c:c + 256], e_ref[...]) for c in range(0, w, 256)], axis=1)


def _mod_kernel(c_ref, w_ref, b_ref, o_ref):
    c = c_ref[...]
    a = c * _sigmoid(c)
    o_ref[0] = _dot(a.astype(BF16), w_ref[0].astype(BF16)) + b_ref[0]


def _modulation(c_rows, w_mod, b_mod):
    n_layers, d, n_out = w_mod.shape
    tn = 1536
    return pl.pallas_call(
        _mod_kernel,
        grid=(n_layers, n_out // tn),
        in_specs=[pl.BlockSpec((8, d), lambda l, j: (0, 0)),
                  pl.BlockSpec((1, d, tn), lambda l, j: (l, 0, j)),
                  pl.BlockSpec((1, 1, tn), lambda l, j: (l, 0, j))],
        out_specs=pl.BlockSpec((1, 8, tn), lambda l, j: (l, 0, j)),
        out_shape=jax.ShapeDtypeStruct((n_layers, 8, n_out), F32),
        compiler_params=_cparams(("parallel", "parallel")),
        name="mod",
    )(c_rows, w_mod, b_mod.reshape(n_layers, 1, n_out))


def _inproj_kernel(x_ref, xp_ref, xn_ref, mod_ref, g1_ref, w_ref, tab_ref, e32_ref, e64_ref, e128_ref,
                   vec_ref, wuq_ref, wuk_ref, wuv_ref,
                   dq_ref, dkt_ref, dv_ref, mq_ref, mkt_ref, mv_ref,
                   lq_ref, lk_ref, lv_ref, lo_ref, lg_ref, nq_ref, nk_ref, nv_ref, *, n_lat, n_tiles):
    i = pl.program_id(0)
    tm = x_ref.shape[1]
    g1 = g1_ref[...]
    shift = mod_ref[0, 0, 0:1, :]
    scale = mod_ref[0, 0, 1:2, :]

    def normmod(xv):
        ms = jnp.mean(xv * xv, axis=-1, keepdims=True)
        return ((xv * lax.rsqrt(ms + NORM_EPS)) * g1 * scale + shift).astype(BF16)

    h = normmod(x_ref[0])

    def proj(c0, width):
        return _dot(h, w_ref[:, c0:c0 + width])

    def vec(r, width):
        return vec_ref[r:r + 1, 0:width]

    tab = tab_ref[...]
    cos128, sa128, sb128 = tab[:, 0:128], tab[:, 128:256], tab[:, 256:384]

    def rope(t, cos, sa, sb):
        w = t.shape[1]
        return t * cos + pltpu.roll(t, w - 8, axis=1) * sa + pltpu.roll(t, 8, axis=1) * sb

    def tile_lanes(t, n):
        return jnp.concatenate([t] * n, axis=1)

    cos_d, sa_d, sb_d = tile_lanes(cos128, 2), tile_lanes(sa128, 2), tile_lanes(sb128, 2)
    xq = proj(C_DQ, 256)
    qn = xq * lax.rsqrt(_group_sumsq(xq, e32_ref) * (1.0 / DIFF_DK) + NORM_EPS) * vec(V_DGQ, 256)
    dq_ref[0] = rope(qn, cos_d, sa_d, sb_d).astype(BF16)
    xk = proj(C_DK, 256)
    kn = xk * lax.rsqrt(_group_sumsq(xk, e32_ref) * (1.0 / DIFF_DK) + NORM_EPS) * vec(V_DGK, 256)
    dkt_ref[0] = rope(kn, cos_d, sa_d, sb_d).T.astype(BF16)
    dv_ref[0] = proj(C_DV, 256).astype(BF16)

    lane128 = lax.broadcasted_iota(jnp.int32, (tm, 128), 1)
    in_rope = (lane128 >= MLA_NOPE) & (lane128 < MLA_DK)
    cos_m = tile_lanes(jnp.where(in_rope, cos128, 1.0), 4)
    sa_m = tile_lanes(jnp.where(in_rope, sa128, 0.0), 4)
    sb_m = tile_lanes(jnp.where(in_rope, sb128, 0.0), 4)
    cq = proj(C_CQ, 256)
    cqn = cq * lax.rsqrt(jnp.mean(cq * cq, axis=-1, keepdims=True) + NORM_EPS) * vec(V_GCQ, 256)
    q = _dot(cqn.astype(BF16), wuq_ref[...])
    q = q * lax.rsqrt(_group_sumsq(q, e128_ref) * (1.0 / MLA_DK) + NORM_EPS) * vec(V_MGQ, 512)
    mq_ref[0] = rope(q, cos_m, sa_m, sb_m).astype(BF16)
    ckv = proj(C_CKV, 128)
    ckvn = (ckv * lax.rsqrt(jnp.mean(ckv * ckv, axis=-1, keepdims=True) + NORM_EPS) * vec(V_GCKV, 128)).astype(BF16)
    k = _dot(ckvn, wuk_ref[...]) + tile_lanes(proj(C_KR, 128), 4)
    k = k * lax.rsqrt(_group_sumsq(k, e128_ref) * (1.0 / MLA_DK) + NORM_EPS) * vec(V_MGK, 512)
    mkt_ref[0] = rope(k, cos_m, sa_m, sb_m).T.astype(BF16)
    mv_ref[0] = _dot(ckvn, wuv_ref[...]).astype(BF16)

    u = proj(C_LQ, 512)
    w_lqk = w_ref[:, C_LQ:C_LQ + 512]
    u_prev = _dot(normmod(xp_ref[0]), w_lqk)[7:8, :]
    u_next = _dot(normmod(xn_ref[0]), w_lqk)[0:1, :]
    has_prev = jnp.where((i != 0) & (i != n_lat), 1.0, 0.0)
    has_next = jnp.where((i != n_lat - 1) & (i != n_tiles - 1), 1.0, 0.0)
    row = lax.broadcasted_iota(jnp.int32, (tm, 512), 0)
    up = jnp.where(row == 0, u_prev * has_prev, pltpu.roll(u, 1, axis=0))
    dn = jnp.where(row == tm - 1, u_next * has_next, pltpu.roll(u, tm - 1, axis=0))
    conv = vec(V_CW, 512) * up + vec(V_CW + 1, 512) * u + vec(V_CW + 2, 512) * dn
    qk = conv * _sigmoid(conv)
    lq_ref[0] = qk[:, 0:256].astype(BF16)
    lk_ref[0] = (qk[:, 256:512] * (HEAD_DIM ** -0.5)).astype(BF16)
    lv_ref[0] = proj(C_LV, 256).astype(BF16)
    lo_ref[0] = _sigmoid(proj(C_LO, 256)).astype(BF16)
    gt = proj(C_LG, 128) + vec(V_GB, 128)
    log_sig = jnp.minimum(gt, 0.0) - jnp.log1p(jnp.exp(-jnp.abs(gt)))
    lg_ref[0] = jnp.where(lane128 < 2 * N_HEADS, gt, log_sig)

    xq = proj(C_NQ, 256)
    nq_ref[0] = (xq * lax.rsqrt(_group_sumsq(xq, e64_ref) * (1.0 / HEAD_DIM) + NORM_EPS) * vec(V_NGQ, 256)).astype(BF16)
    xk = proj(C_NK, 256)
    nk_ref[0] = (xk * lax.rsqrt(_group_sumsq(xk, e64_ref) * (1.0 / HEAD_DIM) + NORM_EPS) * vec(V_NGK, 256)).astype(BF16)
    nv_ref[0] = proj(C_NV, 256).astype(BF16)


def _inproj(x_all, modv, g1, w1, tab, consts, vecs, wuq, wuk, wuv, n_lat):
    bsz, tt, d = x_all.shape
    tm = ROW_TILE
    n_tiles = tt // tm
    last8 = tt // 8 - 1

    def full(a):
        return pl.BlockSpec(a.shape, lambda i, b: (0,) * a.ndim)

    def tok(width):
        return pl.BlockSpec((1, tm, width), lambda i, b: (b, i, 0))

    def tok_t(rows):
        return pl.BlockSpec((1, rows, tm), lambda i, b: (b, 0, i))

    def sds(shape, dt=BF16):
        return jax.ShapeDtypeStruct(shape, dt)

    e32, e64, e128 = consts
    outs = [
        (sds((bsz, tt, 256)), tok(256)), (sds((bsz, 256, tt)), tok_t(256)), (sds((bsz, tt, 256)), tok(256)),
        (sds((bsz, tt, 512)), tok(512)), (sds((bsz, 512, tt)), tok_t(512)), (sds((bsz, tt, 256)), tok(256)),
        (sds((bsz, tt, 256)), tok(256)), (sds((bsz, tt, 256)), tok(256)), (sds((bsz, tt, 256)), tok(256)),
        (sds((bsz, tt, 256)), tok(256)), (sds((bsz, tt, 128), F32), tok(128)),
        (sds((bsz, tt, 256)), tok(256)), (sds((bsz, tt, 256)), tok(256)), (sds((bsz, tt, 256)), tok(256)),
    ]
    return pl.pallas_call(
        functools.partial(_inproj_kernel, n_lat=n_lat, n_tiles=n_tiles),
        grid=(n_tiles, bsz),
        in_specs=[
            pl.BlockSpec((1, tm, d), lambda i, b: (b, i, 0)),
            pl.BlockSpec((1, 8, d), lambda i, b: (b, jnp.maximum(i * (tm // 8) - 1, 0), 0)),
            pl.BlockSpec((1, 8, d), lambda i, b: (b, jnp.minimum((i + 1) * (tm // 8), last8), 0)),
            pl.BlockSpec((1, 1, 8, d), lambda i, b: (b, (i >= n_lat).astype(jnp.int32), 0, 0)),
            full(g1), full(w1),
            pl.BlockSpec((tm, 384), lambda i, b: (i, 0)),
            full(e32), full(e64), full(e128), full(vecs), full(wuq), full(wuk), full(wuv),
        ],
        out_specs=[o[1] for o in outs],
        out_shape=[o[0] for o in outs],
        compiler_params=_cparams(("parallel", "parallel")),
        name="inproj",
    )(x_all, x_all, x_all, modv, g1, w1, tab, e32, e64, e128, vecs, wuq, wuk, wuv)


def _attn_kernel(q_ref, k_ref, v_ref, lam_ref, gout_ref, o_ref, q_scr, m_scr, l_scr, acc_scr, *,
                 n_maps, dk, transposed, lam_init):
    ki = pl.program_id(2)
    n_hm = N_HEADS * n_maps
    tq = q_ref.shape[1]
    lane_head = lax.broadcasted_iota(jnp.int32, (tq, GROUP_W), 1) // HEAD_DIM

    @pl.when(ki == 0)
    def _init():
        for hm in range(n_hm):
            if transposed:
                q_scr[hm] = q_ref[0, :, hm * dk:(hm + 1) * dk]
            else:
                q_scr[hm] = jnp.where(lane_head == hm, q_ref[0], jnp.zeros_like(q_ref[0]))
        m_scr[...] = jnp.full(m_scr.shape, NEG, F32)
        l_scr[...] = jnp.zeros(l_scr.shape, F32)
        acc_scr[...] = jnp.zeros(acc_scr.shape, F32)

    v = v_ref[0]
    for hm in range(n_hm):
        if transposed:
            s = _dot(q_scr[hm], k_ref[0, hm * dk:(hm + 1) * dk, :])
        else:
            s = _dot_nt(q_scr[hm], k_ref[0])
        m_prev = m_scr[hm]
        m_new = jnp.maximum(m_prev, jnp.max(s, axis=-1, keepdims=True))
        alpha = jnp.exp2(m_prev - m_new)
        p = jnp.exp2(s - m_new)
        l_scr[hm] = alpha * l_scr[hm] + jnp.sum(p, axis=-1, keepdims=True)
        acc_scr[hm] = alpha * acc_scr[hm] + _dot(p.astype(BF16), v)
        m_scr[hm] = m_new

    @pl.when(ki == pl.num_programs(2) - 1)
    def _finish():
        out = jnp.zeros((tq, GROUP_W), F32)
        if n_maps == 2:
            lv = lam_ref[...]
            lam = (jnp.exp(jnp.sum(lv[0:1] * lv[1:2], axis=-1, keepdims=True))
                   - jnp.exp(jnp.sum(lv[2:3] * lv[3:4], axis=-1, keepdims=True)) + lam_init)
        for h in range(N_HEADS):
            if n_maps == 2:
                o = acc_scr[2 * h] / l_scr[2 * h] - lam * (acc_scr[2 * h + 1] / l_scr[2 * h + 1])
                ms = jnp.sum(jnp.where(lane_head == h, o * o, 0.0), axis=-1, keepdims=True) * (1.0 / HEAD_DIM)
                o = o * lax.rsqrt(ms + NORM_EPS) * gout_ref[...] * (1.0 - lam_init)
            else:
                o = acc_scr[h] / l_scr[h]
            out = jnp.where(lane_head == h, o, out)
        o_ref[0] = out.astype(o_ref.dtype)


def _attention(q, k, v, lam, gout, *, q_off, n_q, tq, k_off, n_k, tk, n_maps, dk, transposed, lam_init):
    bsz = q.shape[0]
    n_hm = N_HEADS * n_maps
    if transposed:
        k_spec = pl.BlockSpec((1, k.shape[1], tk), lambda b, i, j: (b, 0, j + k_off))
        q_scr = pltpu.VMEM((n_hm, tq, dk), BF16)
    else:
        k_spec = pl.BlockSpec((1, tk, k.shape[2]), lambda b, i, j: (b, j + k_off, 0))
        q_scr = pltpu.VMEM((n_hm, tq, GROUP_W), BF16)
    return pl.pallas_call(
        functools.partial(_attn_kernel, n_maps=n_maps, dk=dk, transposed=transposed, lam_init=lam_init),
        grid=(bsz, n_q, n_k),
        in_specs=[
            pl.BlockSpec((1, tq, q.shape[2]), lambda b, i, j: (b, i + q_off, 0)),
            k_spec,
            pl.BlockSpec((1, tk, GROUP_W), lambda b, i, j: (b, j + k_off, 0)),
            pl.BlockSpec(lam.shape, lambda b, i, j: (0, 0)),
            pl.BlockSpec(gout.shape, lambda b, i, j: (0, 0)),
        ],
        out_specs=pl.BlockSpec((1, tq, GROUP_W), lambda b, i, j: (b, i, 0)),
        out_shape=jax.ShapeDtypeStruct((bsz, n_q * tq, GROUP_W), BF16),
        scratch_shapes=[q_scr, pltpu.VMEM((n_hm, tq, 1), F32), pltpu.VMEM((n_hm, tq, 1), F32),
                        pltpu.VMEM((n_hm, tq, GROUP_W), F32)],
        compiler_params=_cparams(("parallel", "parallel", "arbitrary")),
        name="attn",
    )(q, k, v, lam, gout)


def _mlstm_kernel(q_ref, k_ref, v_ref, g_ref, x_ref, tri_ref, o_ref, ct_scr, n_scr, m_scr, *, n_chunk):
    dr = pl.program_id(1)
    j = pl.program_id(2)
    lc = MLSTM_CHUNK

    @pl.when(j == 0)
    def _init():
        ct_scr[...] = jnp.zeros(ct_scr.shape, F32)
        n_scr[...] = jnp.zeros(n_scr.shape, F32)
        m_scr[...] = jnp.zeros(m_scr.shape, F32)

    row = lax.broadcasted_iota(jnp.int32, (lc, GROUP_W), 0)
    lane = lax.broadcasted_iota(jnp.int32, (lc, GROUP_W), 1)
    pos = lane % lc
    head = lane // HEAD_DIM
    sign = 1 - 2 * dr
    causal = (row - pos) * sign >= 0
    eye_t = pos == row
    blockdiag = (lax.broadcasted_iota(jnp.int32, (GROUP_W, GROUP_W), 0) // HEAD_DIM
                 == lax.broadcasted_iota(jnp.int32, (GROUP_W, GROUP_W), 1) // HEAD_DIM)
    ones_bd = jnp.where(blockdiag, 1.0, 0.0).astype(BF16)
    expand = x_ref[0]
    tri = tri_ref[0]

    ct, n_s, m_s = ct_scr[...], n_scr[...], m_scr[...]
    for jj in range(n_chunk):
        c = jnp.where(dr == 0, jj, n_chunk - 1 - jj)
        off = pl.multiple_of(c * lc, lc)
        q = q_ref[0, pl.ds(off, lc), :]
        k = k_ref[0, pl.ds(off, lc), :]
        v = v_ref[0, pl.ds(off, lc), :]
        gates = jnp.dot(g_ref[0, pl.ds(off, lc), :], expand, precision=HIGHEST, preferred_element_type=F32)
        ig, lf = gates[:, 0:GROUP_W], gates[:, GROUP_W:2 * GROUP_W]
        b = jnp.dot(tri, lf, precision=HIGHEST, preferred_element_type=F32)
        g = jnp.sum(lf, axis=0, keepdims=True)
        a = g - b + ig
        m_loc = jnp.max(a, axis=0, keepdims=True)

        r_row = jnp.sum(jnp.where(eye_t, b - ig, 0.0), axis=0, keepdims=True)
        dlog = jnp.where(causal, b - r_row, NEG)
        inter = b + m_s
        m_t = inter
        for h in range(N_HEADS):
            mh = jnp.max(jnp.where(head == h, dlog, NEG), axis=1, keepdims=True)
            m_t = jnp.where(head == h, jnp.maximum(m_t, mh), m_t)
        k_bd = jnp.where(blockdiag, jnp.concatenate([k] * N_HEADS, axis=0), jnp.zeros((), BF16))
        v_bd = jnp.where(blockdiag, jnp.concatenate([v] * N_HEADS, axis=0), jnp.zeros((), BF16))
        s = _dot_nt(q, k_bd) * jnp.exp(dlog - m_t)
        e = jnp.exp(inter - m_t)
        s_hi = s.astype(BF16)
        s_lo = (s - s_hi.astype(F32)).astype(BF16)
        num = _dot(s_hi, v_bd) + e * _dot(q, ct.astype(BF16))
        den = _dot(s_hi, ones_bd) + _dot(s_lo, ones_bd) + e * _dot_hilo(q.astype(F32) * n_s, ones_bd)
        o_ref[0, 0, pl.ds(off, lc), :] = num / jnp.maximum(jnp.abs(den), jnp.exp(-m_t))

        m_new = jnp.maximum(g + m_s, m_loc)
        a_old = jnp.exp(g + m_s - m_new)
        kw = k.astype(F32) * jnp.exp(a - m_new)
        ct = a_old * ct + jnp.where(blockdiag, _dot_tn(kw.astype(BF16), v), 0.0)
        n_s = a_old * n_s + jnp.sum(kw, axis=0, keepdims=True)
        m_s = m_new
    ct_scr[...] = ct
    n_scr[...] = n_s
    m_scr[...] = m_s


def _mlstm(lq, lk, lv, lg, expand, tri, n_lat_tok):
    bsz, tt, _ = lq.shape
    tb = ROW_TILE
    nb = tt // tb
    nbl = n_lat_tok // tb
    nbc = nb - nbl

    def blk(dr, j):
        jc = jnp.where(dr == 0, j, nbc - 1 - j) + nbl
        jl = jnp.where(dr == 0, j - nbc, nb - 1 - j)
        return jnp.where(j < nbc, jc, jl)

    def tok(width):
        return pl.BlockSpec((1, tb, width), lambda b, dr, j: (b, blk(dr, j), 0))

    return pl.pallas_call(
        functools.partial(_mlstm_kernel, n_chunk=tb // MLSTM_CHUNK),
        grid=(bsz, 2, nb),
        in_specs=[tok(GROUP_W), tok(GROUP_W), tok(GROUP_W), tok(128),
                  pl.BlockSpec((1, 128, 2 * GROUP_W), lambda b, dr, j: (dr, 0, 0)),
                  pl.BlockSpec((1, MLSTM_CHUNK, MLSTM_CHUNK), lambda b, dr, j: (dr, 0, 0))],
        out_specs=pl.BlockSpec((1, 1, tb, GROUP_W), lambda b, dr, j: (dr, b, blk(dr, j), 0)),
        out_shape=jax.ShapeDtypeStruct((2, bsz, tt, GROUP_W), F32),
        scratch_shapes=[pltpu.VMEM((GROUP_W, GROUP_W), F32), pltpu.VMEM((1, GROUP_W), F32),
                        pltpu.VMEM((1, GROUP_W), F32)],
        compiler_params=_cparams(("parallel", "parallel", "arbitrary")),
        name="mlstm",
    )(lq, lk, lv, lg, expand, tri)


def _nbr_kernel(q_ref, kp_ref, kc_ref, kn_ref, vp_ref, vc_ref, vn_ref, kx_ref, vx_ref, bias_ref, o_ref,
                kw_scr, vw_scr, *, n_rows):
    i = pl.program_id(1)
    blk = NA_ROWS * GRID_W
    kw_scr[0:blk] = kp_ref[0]
    kw_scr[blk:2 * blk] = kc_ref[0]
    kw_scr[2 * blk:3 * blk] = kn_ref[0]
    vw_scr[0:blk] = vp_ref[0]
    vw_scr[blk:2 * blk] = vc_ref[0]
    vw_scr[2 * blk:3 * blk] = vn_ref[0]
    kx = kx_ref[0]
    vx = vx_ref[0]
    lane_head = lax.broadcasted_iota(jnp.int32, (GRID_W, GROUP_W), 1) // HEAD_DIM

    def row_body(rr, carry):
        r = i * NA_ROWS + rr
        rs = jnp.clip(r - NA_ROWS // 2, 0, n_rows - NA_ROWS)
        off = pl.multiple_of((rs - (i - 1) * NA_ROWS) * GRID_W, GRID_W)
        delta = r - rs
        qoff = pl.multiple_of(rr * GRID_W, GRID_W)
        q = q_ref[0, pl.ds(qoff, GRID_W), :]
        kw = kw_scr[pl.ds(off, blk), :]
        vw = vw_scr[pl.ds(off, blk), :]
        out = jnp.zeros((GRID_W, GROUP_W), F32)
        for h in range(N_HEADS):
            qh = jnp.where(lane_head == h, q, jnp.zeros_like(q))
            sw = _dot_nt(qh, kw) + bias_ref[h, delta]
            sx = _dot_nt(qh, kx)
            m = jnp.maximum(jnp.max(sw, axis=-1, keepdims=True), jnp.max(sx, axis=-1, keepdims=True))
            pw = jnp.exp2(sw - m)
            px = jnp.exp2(sx - m)
            l = jnp.sum(pw, axis=-1, keepdims=True) + jnp.sum(px, axis=-1, keepdims=True)
            o = _dot(pw.astype(BF16), vw) + _dot(px.astype(BF16), vx)
            out = jnp.where(lane_head == h, o / l, out)
        o_ref[0, pl.ds(qoff, GRID_W), :] = out.astype(o_ref.dtype)
        return carry

    lax.fori_loop(0, NA_ROWS, row_body, 0)


def _neighbourhood(nq, nk, nv, bias, n_lat_tok, n_ctx_tok):
    bsz = nq.shape[0]
    n_rows = n_lat_tok // GRID_W
    blk = NA_ROWS * GRID_W
    nblk = n_lat_tok // blk
    ctx_blk = n_lat_tok // n_ctx_tok

    def win(shift):
        return pl.BlockSpec((1, blk, GROUP_W), lambda b, i: (b, jnp.clip(i + shift, 0, nblk - 1), 0))

    ctx_spec = pl.BlockSpec((1, n_ctx_tok, GROUP_W), lambda b, i: (b, ctx_blk, 0))
    return pl.pallas_call(
        functools.partial(_nbr_kernel, n_rows=n_rows),
        grid=(bsz, nblk),
        in_specs=[win(0), win(-1), win(0), win(1), win(-1), win(0), win(1), ctx_spec, ctx_spec,
                  pl.BlockSpec(bias.shape, lambda b, i: (0, 0, 0, 0))],
        out_specs=pl.BlockSpec((1, blk, GROUP_W), lambda b, i: (b, i, 0)),
        out_shape=jax.ShapeDtypeStruct((bsz, n_lat_tok, GROUP_W), BF16),
        scratch_shapes=[pltpu.VMEM((3 * blk, GROUP_W), BF16), pltpu.VMEM((3 * blk, GROUP_W), BF16)],
        compiler_params=_cparams(("parallel", "parallel")),
        name="nbr",
    )(nq, nk, nk, nk, nv, nv, nv, nk, nv, bias)


def _outproj_kernel(x_ref, mod_ref, yd_ref, ym_ref, hl_ref, lo_ref, yn_ref, gml_ref, g2_ref, w_ref, e64_ref,
                    x1_ref, h2_ref):
    hs = hl_ref[0, 0] + hl_ref[1, 0]
    ms = _group_sumsq(hs, e64_ref) * (1.0 / HEAD_DIM)
    yl = (hs * lax.rsqrt(ms + NORM_EPS) * gml_ref[...] * lo_ref[0].astype(F32)).astype(BF16)
    o = (_dot(yd_ref[0], w_ref[0:256, :]) + _dot(ym_ref[0], w_ref[256:512, :])
         + _dot(yl, w_ref[512:768, :]) + _dot(yn_ref[0], w_ref[768:1024, :]))
    x1 = x_ref[0] + mod_ref[0, 0, 2:3, :] * o
    x1_ref[0] = x1
    ms2 = jnp.mean(x1 * x1, axis=-1, keepdims=True)
    h2 = (x1 * lax.rsqrt(ms2 + NORM_EPS)) * g2_ref[...] * mod_ref[0, 0, 4:5, :] + mod_ref[0, 0, 3:4, :]
    h2_ref[0] = h2.astype(BF16)


def _outproj(x_all, modv, yd, ym, hl, lo, yn, gml, g2, w_out, e64, n_lat, n_tiles):
    bsz, _, d = x_all.shape
    tm = ROW_TILE

    def tok(width):
        return pl.BlockSpec((1, tm, width), lambda i, b: (b, i, 0))

    def full(a):
        return pl.BlockSpec(a.shape, lambda i, b: (0,) * a.ndim)

    return pl.pallas_call(
        _outproj_kernel,
        grid=(n_tiles, bsz),
        in_specs=[tok(d), pl.BlockSpec((1, 1, 8, d), lambda i, b: (b, (i >= n_lat).astype(jnp.int32), 0, 0)),
                  tok(GROUP_W), tok(GROUP_W),
                  pl.BlockSpec((2, 1, tm, GROUP_W), lambda i, b: (0, b, i, 0)),
                  tok(GROUP_W), tok(GROUP_W), full(gml), full(g2), full(w_out), full(e64)],
        out_specs=[tok(d), tok(d)],
        out_shape=[jax.ShapeDtypeStruct((bsz, n_tiles * tm, d), F32),
                   jax.ShapeDtypeStruct((bsz, n_tiles * tm, d), BF16)],
        compiler_params=_cparams(("parallel", "parallel")),
        name="outproj",
    )(x_all, modv, yd, ym, hl, lo, yn, gml, g2, w_out, e64)


def _ffn_kernel(x1_ref, h_ref, hp_ref, hn_ref, mod_ref, wup_ref, wc_ref, wdn_ref, o_ref, *, n_lat, n_tiles):
    i = pl.program_id(0)
    tm = h_ref.shape[1]
    h = h_ref[0]
    hp = hp_ref[0]
    hn = hn_ref[0]
    has_prev = jnp.where((i != 0) & (i != n_lat), 1.0, 0.0)
    has_next = jnp.where((i != n_lat - 1) & (i != n_tiles - 1), 1.0, 0.0)
    row = lax.broadcasted_iota(jnp.int32, (tm, FFN_CHUNK), 0)
    first, last = row == 0, row == tm - 1

    def conv_up(c0):
        w = wup_ref[:, c0:c0 + FFN_CHUNK]
        u = _dot(h, w)
        u_prev = _dot(hp, w)[FFN_HALO - 1:FFN_HALO, :] * has_prev
        u_next = _dot(hn, w)[0:1, :] * has_next
        up = jnp.where(first, u_prev, pltpu.roll(u, 1, axis=0))
        dn = jnp.where(last, u_next, pltpu.roll(u, tm - 1, axis=0))
        return (wc_ref[0:1, c0:c0 + FFN_CHUNK] * up + wc_ref[1:2, c0:c0 + FFN_CHUNK] * u
                + wc_ref[2:3, c0:c0 + FFN_CHUNK] * dn)

    acc = jnp.zeros((tm, x1_ref.shape[2]), F32)
    for c in range(FFN_HIDDEN // FFN_CHUNK):
        a = conv_up(c * FFN_CHUNK)
        g = conv_up(FFN_HIDDEN + c * FFN_CHUNK)
        act = (g * _sigmoid(g) * a).astype(BF16)
        acc = acc + _dot(act, wdn_ref[c * FFN_CHUNK:(c + 1) * FFN_CHUNK, :])
    o_ref[0] = x1_ref[0] + mod_ref[0, 0, 5:6, :] * acc


def _ffn(x1, h2, modv, w_up, w_conv, w_down, n_lat):
    bsz, rows, d = x1.shape
    tm = ROW_TILE
    n_tiles = rows // tm
    per = tm // FFN_HALO
    last = rows // FFN_HALO - 1

    def tok():
        return pl.BlockSpec((1, tm, d), lambda i, b: (b, i, 0))

    def full(a):
        return pl.BlockSpec(a.shape, lambda i, b: (0,) * a.ndim)

    return pl.pallas_call(
        functools.partial(_ffn_kernel, n_lat=n_lat, n_tiles=n_tiles),
        grid=(n_tiles, bsz),
        in_specs=[tok(), tok(),
                  pl.BlockSpec((1, FFN_HALO, d), lambda i, b: (b, jnp.maximum(i * per - 1, 0), 0)),
                  pl.BlockSpec((1, FFN_HALO, d), lambda i, b: (b, jnp.minimum((i + 1) * per, last), 0)),
                  pl.BlockSpec((1, 1, 8, d), lambda i, b: (b, (i >= n_lat).astype(jnp.int32), 0, 0)),
                  full(w_up), full(w_conv), full(w_down)],
        out_specs=tok(),
        out_shape=jax.ShapeDtypeStruct((bsz, rows, d), F32),
        compiler_params=_cparams(("parallel", "parallel")),
        name="ffn",
    )(x1, h2, h2, h2, modv, w_up, w_conv, w_down)


def _block_ones(width, group):
    idx = np.arange(width) // group
    return jnp.asarray(idx[:, None] == idx[None, :], dtype=BF16)


def _rope_table(t_len, n_ctx):
    nf = MLA_ROPE // 4
    t = jnp.arange(t_len, dtype=jnp.int32)
    row = (t // GRID_W).astype(F32)
    col = (t % GRID_W).astype(F32)
    inv = ROPE_BASE ** (-jnp.arange(nf, dtype=F32) / nf)
    ar, ac = row[:, None] * inv, col[:, None] * inv
    cr, sr, cc, sc = jnp.cos(ar), jnp.sin(ar), jnp.cos(ac), jnp.sin(ac)
    z = jnp.zeros_like(sr)
    cos = jnp.tile(jnp.concatenate([cr, cr, cc, cc], -1), (1, 4))
    sa = jnp.tile(jnp.concatenate([-sr, z, -sc, z], -1), (1, 4))
    sb = jnp.tile(jnp.concatenate([z, sr, z, sc], -1), (1, 4))
    lat = jnp.concatenate([cos, sa, sb], -1)
    ctx = jnp.concatenate([jnp.ones((n_ctx, 128), F32), jnp.zeros((n_ctx, 256), F32)], -1)
    return jnp.concatenate([lat, ctx], 0)


def _pack_w_in(w):
    d = w.shape[0]
    z = lambda n: jnp.zeros((d, n), w.dtype)
    return jnp.concatenate([w[:, 0:1152], z(64), w[:, 1152:1184], z(32), w[:, 1184:2208],
                            w[:, 2208:2224], z(112), w[:, 2224:2992]], axis=1).astype(BF16)


def _pad_heads(v, real, padded):
    lead = v.shape[:-1]
    v = v.reshape(lead + (N_HEADS, real))
    v = jnp.pad(v, [(0, 0)] * len(lead) + [(0, 0), (0, padded - real)])
    return v.reshape(lead + (N_HEADS * padded,))


def _pack_vectors(dgq, dgk, gcq, gckv, mgq, mgk, ngq, ngk, b_i, b_f, w_conv):
    def row(v):
        return jnp.pad(v.astype(F32), (0, 512 - v.shape[0]))[None]

    rows = [
        row(jnp.tile(dgq, 8) * (DIFF_DK ** -0.5 * LOG2E)), row(jnp.tile(dgk, 8)),
        row(gcq), row(gckv),
        row(_pad_heads(jnp.tile(mgq, N_HEADS), MLA_DK, MLA_PAD) * (MLA_DK ** -0.5 * LOG2E)),
        row(_pad_heads(jnp.tile(mgk, N_HEADS), MLA_DK, MLA_PAD)),
        row(jnp.tile(ngq, N_HEADS) * (HEAD_DIM ** -0.5 * LOG2E)), row(jnp.tile(ngk, N_HEADS)),
        row(jnp.concatenate([b_i.reshape(-1), b_f.reshape(-1)])),
        w_conv.astype(F32),
    ]
    rows = jnp.concatenate(rows, axis=0)
    return jnp.pad(rows, ((0, 16 - rows.shape[0]), (0, 0)))


def _nbr_bias(rpb):
    j = np.arange(NA_ROWS)
    delta = np.arange(NA_ROWS)
    roff = j[None, :] - delta[:, None] + (NA_ROWS - 1)
    cidx = np.arange(GRID_W)
    coff = np.clip(cidx[None, :] - cidx[:, None], 1 - NA_COLS, NA_COLS - 1) + (NA_COLS - 1)
    cs = np.clip(cidx - NA_COLS // 2, 0, GRID_W - NA_COLS)
    col_ok = (cidx[None, :] >= cs[:, None]) & (cidx[None, :] < cs[:, None] + NA_COLS)
    bias = rpb[:, roff[:, None, :, None], coff[None, :, None, :]]
    bias = jnp.where(jnp.asarray(col_ok)[None, None, :, None, :], bias.astype(F32) * LOG2E, NEG)
    return bias.reshape(N_HEADS, NA_ROWS, GRID_W, NA_ROWS * GRID_W)


def _mlstm_consts():
    expand = np.zeros((2, 128, 2 * GROUP_W), np.float32)
    for dr in range(2):
        for h in range(N_HEADS):
            expand[dr, dr * N_HEADS + h, h * HEAD_DIM:(h + 1) * HEAD_DIM] = 1.0
            expand[dr, 2 * N_HEADS + dr * N_HEADS + h, GROUP_W + h * HEAD_DIM:GROUP_W + (h + 1) * HEAD_DIM] = 1.0
    lower = np.tril(np.ones((MLSTM_CHUNK, MLSTM_CHUNK), np.float32))
    return jnp.asarray(expand), jnp.asarray(np.stack([lower, lower.T]))


def _kv_tile(total):
    best = 128
    for t in range(128, 1025, 128):
        if total % t == 0:
            best = t
    return best


def kernel(x, c, ctx, c_ctx, w_mod, b_mod, g_norm1, g_norm2, w_in, w_out, diff_g_q, diff_g_k, diff_lam, diff_g_out, mla_g_cq, mla_g_ckv, mla_w_uq, mla_w_ukv, mla_g_q, mla_g_k, mlstm_w_conv, mlstm_b_i, mlstm_b_f, mlstm_g_out, na_g_q, na_g_k, na_rpb, ffn_w_up, ffn_w_conv, ffn_w_down):
    bsz, t_lat, d = x.shape
    n_ctx = ctx.shape[1]
    depth = w_in.shape[0]
    tt = t_lat + n_ctx
    tm = ROW_TILE
    assert d == D_MODEL and t_lat % (NA_ROWS * GRID_W) == 0 and n_ctx % tm == 0 and bsz < 8
    n_lat, n_tiles = t_lat // tm, tt // tm

    c_rows = jnp.zeros((8, d), F32).at[:bsz].set(c).at[bsz].set(c_ctx)
    mod_all = _modulation(c_rows, w_mod, b_mod)
    tab = _rope_table(t_lat, n_ctx)
    consts = (_block_ones(256, DIFF_DK), _block_ones(256, HEAD_DIM), _block_ones(256, MLA_PAD))
    expand, tri = _mlstm_consts()
    tk_lat = _kv_tile(tt)

    x_all = jnp.concatenate([x, ctx], axis=1)
    for l in range(depth):
        need_ctx = l < depth - 1
        lam_init = 0.8 - 0.6 * math.exp(-0.3 * l)
        m = mod_all[l].reshape(8, 6, d)
        lat = jnp.pad(m[:bsz], ((0, 0), (0, 2), (0, 0)))
        cx = jnp.broadcast_to(jnp.pad(m[bsz], ((0, 2), (0, 0)))[None], lat.shape)
        modv = jnp.stack([lat, cx], axis=1)
        modv = modv.at[:, :, 1].add(1.0).at[:, :, 4].add(1.0)

        wukv = mla_w_ukv[l].reshape(MLA_KV_LORA, N_HEADS, MLA_NOPE + HEAD_DIM)
        wuk = jnp.pad(wukv[:, :, :MLA_NOPE], ((0, 0), (0, 0), (0, MLA_PAD - MLA_NOPE))).reshape(MLA_KV_LORA, -1)
        wuv = wukv[:, :, MLA_NOPE:].reshape(MLA_KV_LORA, -1)
        vecs = _pack_vectors(diff_g_q[l], diff_g_k[l], mla_g_cq[l], mla_g_ckv[l], mla_g_q[l], mla_g_k[l],
                             na_g_q[l], na_g_k[l], mlstm_b_i[l], mlstm_b_f[l], mlstm_w_conv[l])
        (dq, dkt, dv, mq, mkt, mv, lq, lk, lv, lo, lg, nq, nk, nv) = _inproj(
            x_all, modv, g_norm1[l][None], _pack_w_in(w_in[l]), tab, consts, vecs,
            _pad_heads(mla_w_uq[l], MLA_DK, MLA_PAD).astype(BF16), wuk.astype(BF16), wuv.astype(BF16), n_lat)

        lam = diff_lam[l].astype(F32)
        gout = jnp.tile(diff_g_out[l].astype(F32), N_HEADS)[None]
        lat_q = dict(q_off=0, n_q=n_lat, tq=tm, k_off=0, n_k=tt // tk_lat, tk=tk_lat)
        ctx_q = dict(q_off=n_lat, n_q=n_ctx // tm, tq=tm, k_off=t_lat // n_ctx, n_k=1, tk=n_ctx)
        diff_cfg = dict(n_maps=2, dk=DIFF_DK, transposed=True, lam_init=lam_init)
        mla_cfg = dict(n_maps=1, dk=MLA_PAD, transposed=True, lam_init=0.0)
        yd = _attention(dq, dkt, dv, lam, gout, **lat_q, **diff_cfg)
        ym = _attention(mq, mkt, mv, lam, gout, **lat_q, **mla_cfg)
        yn = _neighbourhood(nq, nk, nv, _nbr_bias(na_rpb[l]), t_lat, n_ctx)
        hl = _mlstm(lq, lk, lv, lg, expand, tri, t_lat)
        if need_ctx:
            na_cfg = dict(n_maps=1, dk=HEAD_DIM, transposed=False, lam_init=0.0)
            yd = jnp.concatenate([yd, _attention(dq, dkt, dv, lam, gout, **ctx_q, **diff_cfg)], axis=1)
            ym = jnp.concatenate([ym, _attention(mq, mkt, mv, lam, gout, **ctx_q, **mla_cfg)], axis=1)
            yn = jnp.concatenate([yn, _attention(nq, nk, nv, lam, gout, **ctx_q, **na_cfg)], axis=1)
        rows = n_tiles if need_ctx else n_lat
        x1, h2 = _outproj(x_all, modv, yd, ym, hl, lo, yn, jnp.tile(mlstm_g_out[l].astype(F32), N_HEADS)[None],
                          g_norm2[l][None], w_out[l].astype(BF16), consts[1], n_lat, rows)
        x_all = _ffn(x1, h2, modv, ffn_w_up[l].astype(BF16), ffn_w_conv[l].astype(F32),
                     ffn_w_down[l].astype(BF16), n_lat)
    return x_all[:, :t_lat]
```

```python
import functools
import math

import numpy as np
import jax
import jax.numpy as jnp
from jax import lax
from jax.experimental import pallas as pl
from jax.experimental.pallas import tpu as pltpu

F32 = jnp.float32
BF16 = jnp.bfloat16
HIGHEST = lax.Precision.HIGHEST

D_MODEL = 1024
GRID_W = 64
N_HEADS = 4
HEAD_DIM = 64
GROUP_W = N_HEADS * HEAD_DIM
DIFF_DK = 32
MLA_Q_LORA = 256
MLA_KV_LORA = 128
MLA_NOPE = 64
MLA_ROPE = 32
MLA_DK = MLA_NOPE + MLA_ROPE
MLA_PAD = 128
DIFF_KS = 64
BOUND_LIMIT = 50.0
MLSTM_CHUNK = 64
NA_ROWS = 8
NA_COLS = 16
FFN_HIDDEN = 2816
ROPE_BASE = 10000.0
NORM_EPS = 1e-6
LOG2E = 1.4426950408889634
NEG = -1e30

ROW_TILE = 256
ATTN_TQ = 512
FFN_CHUNK = 256
FFN_HALO = 16
VMEM_LIMIT = 56 * 1024 * 1024

C_DQ, C_DK, C_DV, C_CQ, C_CKV, C_KR = 0, 256, 512, 768, 1024, 1152
C_LQ, C_LV, C_LO, C_LG, C_NQ, C_NK, C_NV = 1280, 1792, 2048, 2304, 2432, 2688, 2944
W1_COLS = 3200
V_DGQ, V_DGK, V_GCQ, V_GCKV, V_MGQ, V_MGK, V_NGQ, V_NGK, V_GB, V_CW = 0, 1, 2, 3, 4, 5, 6, 7, 8, 9


def _cparams(sem):
    return pltpu.CompilerParams(dimension_semantics=sem, vmem_limit_bytes=VMEM_LIMIT)


def _dot(a, b):
    return jnp.dot(a, b, preferred_element_type=F32)


def _dot_nt(a, b):
    return lax.dot_general(a, b, (((1,), (1,)), ((), ())), preferred_element_type=F32)


def _dot_tn(a, b):
    return lax.dot_general(a, b, (((0,), (0,)), ((), ())), preferred_element_type=F32)


def _dot_hilo(x, w):
    hi = x.astype(BF16)
    lo = (x - hi.astype(F32)).astype(BF16)
    return _dot(hi, w) + _dot(lo, w)


def _sigmoid(x):
    return 1.0 / (1.0 + jnp.exp(-x))


def _group_sumsq(x, e_ref):
    w = x.shape[1]
    sq = (x * x).astype(BF16)
    return jnp.concatenate([_dot(sq[:, c:c + 256], e_ref[...]) for c in range(0, w, 256)], axis=1)


def _mod_kernel(c_ref, w_ref, b_ref, o_ref):
    c = c_ref[...]
    a = c * _sigmoid(c)
    o_ref[0] = _dot(a.astype(BF16), w_ref[0].astype(BF16)) + b_ref[0]


def _modulation(c_rows, w_mod, b_mod):
    n_layers, d, n_out = w_mod.shape
    tn = 1536
    return pl.pallas_call(
        _mod_kernel,
        grid=(n_layers, n_out // tn),
        in_specs=[pl.BlockSpec((8, d), lambda l, j: (0, 0)),
                  pl.BlockSpec((1, d, tn), lambda l, j: (l, 0, j)),
                  pl.BlockSpec((1, 1, tn), lambda l, j: (l, 0, j))],
        out_specs=pl.BlockSpec((1, 8, tn), lambda l, j: (l, 0, j)),
        out_shape=jax.ShapeDtypeStruct((n_layers, 8, n_out), F32),
        compiler_params=_cparams(("parallel", "parallel")),
        name="mod",
    )(c_rows, w_mod, b_mod.reshape(n_layers, 1, n_out))


def _inproj_kernel(x_ref, xp_ref, xn_ref, mod_ref, g1_ref, w_ref, tab_ref, e32_ref, e64_ref, e128_ref,
                   vec_ref, wuq_ref, wuk_ref, wuv_ref,
                   dq_ref, dkt_ref, dv_ref, mq_ref, mkt_ref, mv_ref,
                   lq_ref, lk_ref, lv_ref, lo_ref, lg_ref, nq_ref, nk_ref, nv_ref, dkn_ref, mkn_ref,
                   *, n_lat, n_tiles):
    i = pl.program_id(0)
    tm = x_ref.shape[1]
    g1 = g1_ref[...]
    shift = mod_ref[0, 0, 0:1, :]
    scale = mod_ref[0, 0, 1:2, :]

    def normmod(xv):
        ms = jnp.mean(xv * xv, axis=-1, keepdims=True)
        return ((xv * lax.rsqrt(ms + NORM_EPS)) * g1 * scale + shift).astype(BF16)

    h = normmod(x_ref[0])

    def proj(c0, width):
        return _dot(h, w_ref[:, c0:c0 + width])

    def vec(r, width):
        return vec_ref[r:r + 1, 0:width]

    tab = tab_ref[...]
    cos128, sa128, sb128 = tab[:, 0:128], tab[:, 128:256], tab[:, 256:384]

    def rope(t, cos, sa, sb):
        w = t.shape[1]
        return t * cos + pltpu.roll(t, w - 8, axis=1) * sa + pltpu.roll(t, 8, axis=1) * sb

    def tile_lanes(t, n):
        return jnp.concatenate([t] * n, axis=1)

    cos_d, sa_d, sb_d = tile_lanes(cos128, 2), tile_lanes(sa128, 2), tile_lanes(sb128, 2)
    xq = proj(C_DQ, 256)
    qn = xq * lax.rsqrt(_group_sumsq(xq, e32_ref) * (1.0 / DIFF_DK) + NORM_EPS) * vec(V_DGQ, 256)
    dq_ref[0] = rope(qn, cos_d, sa_d, sb_d).astype(BF16)
    xk = proj(C_DK, 256)
    kn = xk * lax.rsqrt(_group_sumsq(xk, e32_ref) * (1.0 / DIFF_DK) + NORM_EPS) * vec(V_DGK, 256)
    dkn_ref[0, 0] = jnp.broadcast_to(jnp.max(_group_sumsq(kn, e32_ref), axis=0, keepdims=True), (8, 256))
    kt = rope(kn, cos_d, sa_d, sb_d).T
    ones_row = jnp.where(lax.broadcasted_iota(jnp.int32, (DIFF_KS - DIFF_DK, tm), 0) == 0, 1.0, 0.0)
    dkt_ref[0] = jnp.concatenate(
        [piece for mp in range(2 * N_HEADS) for piece in (kt[mp * DIFF_DK:(mp + 1) * DIFF_DK], ones_row)],
        axis=0).astype(BF16)
    dv_ref[0] = proj(C_DV, 256).astype(BF16)

    lane128 = lax.broadcasted_iota(jnp.int32, (tm, 128), 1)
    in_rope = (lane128 >= MLA_NOPE) & (lane128 < MLA_DK)
    cos_m = tile_lanes(jnp.where(in_rope, cos128, 1.0), 4)
    sa_m = tile_lanes(jnp.where(in_rope, sa128, 0.0), 4)
    sb_m = tile_lanes(jnp.where(in_rope, sb128, 0.0), 4)
    cq = proj(C_CQ, 256)
    cqn = cq * lax.rsqrt(jnp.mean(cq * cq, axis=-1, keepdims=True) + NORM_EPS) * vec(V_GCQ, 256)
    q = _dot(cqn.astype(BF16), wuq_ref[...])
    q = q * lax.rsqrt(_group_sumsq(q, e128_ref) * (1.0 / MLA_DK) + NORM_EPS) * vec(V_MGQ, 512)
    mq_ref[0] = rope(q, cos_m, sa_m, sb_m).astype(BF16)
    ckv = proj(C_CKV, 128)
    ckvn = (ckv * lax.rsqrt(jnp.mean(ckv * ckv, axis=-1, keepdims=True) + NORM_EPS) * vec(V_GCKV, 128)).astype(BF16)
    k = _dot(ckvn, wuk_ref[...]) + tile_lanes(proj(C_KR, 128), 4)
    k = k * lax.rsqrt(_group_sumsq(k, e128_ref) * (1.0 / MLA_DK) + NORM_EPS) * vec(V_MGK, 512)
    mkn_ref[0, 0] = jnp.broadcast_to(jnp.max(_group_sumsq(k, e128_ref), axis=0, keepdims=True), (8, 512))
    k = jnp.where(tile_lanes(lane128 == MLA_DK, 4), 1.0, rope(k, cos_m, sa_m, sb_m))
    mkt_ref[0] = k.T.astype(BF16)
    mv_ref[0] = _dot(ckvn, wuv_ref[...]).astype(BF16)

    u = proj(C_LQ, 512)
    w_lqk = w_ref[:, C_LQ:C_LQ + 512]
    u_prev = _dot(normmod(xp_ref[0]), w_lqk)[7:8, :]
    u_next = _dot(normmod(xn_ref[0]), w_lqk)[0:1, :]
    has_prev = jnp.where((i != 0) & (i != n_lat), 1.0, 0.0)
    has_next = jnp.where((i != n_lat - 1) & (i != n_tiles - 1), 1.0, 0.0)
    row = lax.broadcasted_iota(jnp.int32, (tm, 512), 0)
    up = jnp.where(row == 0, u_prev * has_prev, pltpu.roll(u, 1, axis=0))
    dn = jnp.where(row == tm - 1, u_next * has_next, pltpu.roll(u, tm - 1, axis=0))
    conv = vec(V_CW, 512) * up + vec(V_CW + 1, 512) * u + vec(V_CW + 2, 512) * dn
    qk = conv * _sigmoid(conv)
    lq_ref[0] = qk[:, 0:256].astype(BF16)
    lk_ref[0] = (qk[:, 256:512] * (HEAD_DIM ** -0.5)).astype(BF16)
    lv_ref[0] = proj(C_LV, 256).astype(BF16)
    lo_ref[0] = _sigmoid(proj(C_LO, 256)).astype(BF16)
    gt = proj(C_LG, 128) + vec(V_GB, 128)
    log_sig = jnp.minimum(gt, 0.0) - jnp.log1p(jnp.exp(-jnp.abs(gt)))
    lg_ref[0] = jnp.where(lane128 < 2 * N_HEADS, gt, log_sig)

    xq = proj(C_NQ, 256)
    nq_ref[0] = (xq * lax.rsqrt(_group_sumsq(xq, e64_ref) * (1.0 / HEAD_DIM) + NORM_EPS) * vec(V_NGQ, 256)).astype(BF16)
    xk = proj(C_NK, 256)
    nk_ref[0] = (xk * lax.rsqrt(_group_sumsq(xk, e64_ref) * (1.0 / HEAD_DIM) + NORM_EPS) * vec(V_NGK, 256)).astype(BF16)
    nv_ref[0] = proj(C_NV, 256).astype(BF16)


def _inproj(x_all, modv, g1, w1, tab, consts, vecs, wuq, wuk, wuv, n_lat):
    bsz, tt, d = x_all.shape
    tm = ROW_TILE
    n_tiles = tt // tm
    last8 = tt // 8 - 1

    def full(a):
        return pl.BlockSpec(a.shape, lambda i, b: (0,) * a.ndim)

    def tok(width):
        return pl.BlockSpec((1, tm, width), lambda i, b: (b, i, 0))

    def tok_t(rows):
        return pl.BlockSpec((1, rows, tm), lambda i, b: (b, 0, i))

    def sds(shape, dt=BF16):
        return jax.ShapeDtypeStruct(shape, dt)

    e32, e64, e128 = consts
    outs = [
        (sds((bsz, tt, 256)), tok(256)), (sds((bsz, 8 * DIFF_KS, tt)), tok_t(8 * DIFF_KS)), (sds((bsz, tt, 256)), tok(256)),
        (sds((bsz, tt, 512)), tok(512)), (sds((bsz, 512, tt)), tok_t(512)), (sds((bsz, tt, 256)), tok(256)),
        (sds((bsz, tt, 256)), tok(256)), (sds((bsz, tt, 256)), tok(256)), (sds((bsz, tt, 256)), tok(256)),
        (sds((bsz, tt, 256)), tok(256)), (sds((bsz, tt, 128), F32), tok(128)),
        (sds((bsz, tt, 256)), tok(256)), (sds((bsz, tt, 256)), tok(256)), (sds((bsz, tt, 256)), tok(256)),
        (sds((bsz, n_tiles, 8, 256), F32), pl.BlockSpec((1, 1, 8, 256), lambda i, b: (b, i, 0, 0))),
        (sds((bsz, n_tiles, 8, 512), F32), pl.BlockSpec((1, 1, 8, 512), lambda i, b: (b, i, 0, 0))),
    ]
    return pl.pallas_call(
        functools.partial(_inproj_kernel, n_lat=n_lat, n_tiles=n_tiles),
        grid=(n_tiles, bsz),
        in_specs=[
            pl.BlockSpec((1, tm, d), lambda i, b: (b, i, 0)),
            pl.BlockSpec((1, 8, d), lambda i, b: (b, jnp.maximum(i * (tm // 8) - 1, 0), 0)),
            pl.BlockSpec((1, 8, d), lambda i, b: (b, jnp.minimum((i + 1) * (tm // 8), last8), 0)),
            pl.BlockSpec((1, 1, 8, d), lambda i, b: (b, (i >= n_lat).astype(jnp.int32), 0, 0)),
            full(g1), full(w1),
            pl.BlockSpec((tm, 384), lambda i, b: (i, 0)),
            full(e32), full(e64), full(e128), full(vecs), full(wuq), full(wuk), full(wuv),
        ],
        out_specs=[o[1] for o in outs],
        out_shape=[o[0] for o in outs],
        compiler_params=_cparams(("parallel", "parallel")),
        name="inproj",
    )(x_all, x_all, x_all, modv, g1, w1, tab, e32, e64, e128, vecs, wuq, wuk, wuv)


def _attn_kernel(q_ref, k_ref, v_ref, kmax_ref, lam_ref, gout_ref, o_ref, q_scr, m_scr, l_scr, acc_scr, flag_scr, *,
                 n_maps, qs, ks, aug_lane, transposed, lam_init):
    ki = pl.program_id(2)
    n_hm = N_HEADS * n_maps
    tq = q_ref.shape[1]
    tk = v_ref.shape[1]
    lane_head = lax.broadcasted_iota(jnp.int32, (tq, GROUP_W), 1) // HEAD_DIM

    @pl.when(ki == 0)
    def _init():
        if transposed:
            lane = lax.broadcasted_iota(jnp.int32, (tq, ks), 1)
            worst = jnp.zeros((tq, 1), F32)
            for hm in range(n_hm):
                qh = q_ref[0, :, hm * qs:(hm + 1) * qs].astype(F32)
                bound = jnp.sqrt(jnp.sum(qh * qh, axis=-1, keepdims=True)
                                 * kmax_ref[0, :, hm * qs:hm * qs + 1]) * 1.02 + 0.01
                if ks > qs:
                    qh = jnp.concatenate([qh, jnp.zeros((tq, ks - qs), F32)], axis=1)
                q_scr[hm] = jnp.where(lane == aug_lane, -bound, qh).astype(BF16)
                worst = jnp.maximum(worst, bound)
            flag_scr[0] = (jnp.max(worst) > BOUND_LIMIT).astype(jnp.int32)
        else:
            for hm in range(n_hm):
                q_scr[hm] = jnp.where(lane_head == hm, q_ref[0], jnp.zeros_like(q_ref[0]))
            flag_scr[0] = 1
        m_scr[...] = jnp.full(m_scr.shape, NEG, F32)
        l_scr[...] = jnp.zeros(l_scr.shape, F32)
        acc_scr[...] = jnp.zeros(acc_scr.shape, F32)

    def scores(hm):
        if transposed:
            return _dot(q_scr[hm], k_ref[0, hm * ks:(hm + 1) * ks, :])
        return _dot_nt(q_scr[hm], k_ref[0])

    def lane_partial_sums(p):
        return functools.reduce(lambda a, b: a + b, [p[:, c:c + 128] for c in range(0, tk, 128)])

    def single_pass():
        v = v_ref[0]
        for hm in range(n_hm):
            p = jnp.exp2(scores(hm))
            l_scr[hm] += lane_partial_sums(p)
            acc_scr[hm] += _dot(p.astype(BF16), v)

    def online():
        v = v_ref[0]
        for hm in range(n_hm):
            s = scores(hm)
            m_prev = m_scr[hm]
            m_new = jnp.maximum(m_prev, jnp.max(s, axis=-1, keepdims=True))
            alpha = jnp.exp2(m_prev - m_new)
            p = jnp.exp2(s - m_new)
            l_scr[hm] = alpha * l_scr[hm] + lane_partial_sums(p)
            acc_scr[hm] = alpha * acc_scr[hm] + _dot(p.astype(BF16), v)
            m_scr[hm] = m_new

    if transposed:
        pl.when(flag_scr[0] == 0)(single_pass)
        pl.when(flag_scr[0] != 0)(online)
    else:
        online()

    @pl.when(ki == pl.num_programs(2) - 1)
    def _finish():
        out = jnp.zeros((tq, GROUP_W), F32)
        if n_maps == 2:
            lv = lam_ref[...]
            lam = (jnp.exp(jnp.sum(lv[0:1] * lv[1:2], axis=-1, keepdims=True))
                   - jnp.exp(jnp.sum(lv[2:3] * lv[3:4], axis=-1, keepdims=True)) + lam_init)

        def normalised(hm):
            return acc_scr[hm] / jnp.sum(l_scr[hm], axis=-1, keepdims=True)

        for h in range(N_HEADS):
            if n_maps == 2:
                o = normalised(2 * h) - lam * normalised(2 * h + 1)
                ms = jnp.sum(jnp.where(lane_head == h, o * o, 0.0), axis=-1, keepdims=True) * (1.0 / HEAD_DIM)
                o = o * lax.rsqrt(ms + NORM_EPS) * gout_ref[...] * (1.0 - lam_init)
            else:
                o = normalised(h)
            out = jnp.where(lane_head == h, o, out)
        o_ref[0] = out.astype(o_ref.dtype)


def _attention(q, k, v, kmax, lam, gout, *, q_off, n_q, tq, k_off, n_k, tk, n_maps, qs, ks, aug_lane, transposed,
               lam_init):
    bsz = q.shape[0]
    n_hm = N_HEADS * n_maps
    if transposed:
        k_spec = pl.BlockSpec((1, k.shape[1], tk), lambda b, i, j: (b, 0, j + k_off))
        q_scr = pltpu.VMEM((n_hm, tq, ks), BF16)
    else:
        k_spec = pl.BlockSpec((1, tk, k.shape[2]), lambda b, i, j: (b, j + k_off, 0))
        q_scr = pltpu.VMEM((n_hm, tq, GROUP_W), BF16)
    return pl.pallas_call(
        functools.partial(_attn_kernel, n_maps=n_maps, qs=qs, ks=ks, aug_lane=aug_lane, transposed=transposed,
                          lam_init=lam_init),
        grid=(bsz, n_q, n_k),
        in_specs=[
            pl.BlockSpec((1, tq, q.shape[2]), lambda b, i, j: (b, i + q_off, 0)),
            k_spec,
            pl.BlockSpec((1, tk, GROUP_W), lambda b, i, j: (b, j + k_off, 0)),
            pl.BlockSpec((1, 1, kmax.shape[2]), lambda b, i, j: (b, 0, 0)),
            pl.BlockSpec(lam.shape, lambda b, i, j: (0, 0)),
            pl.BlockSpec(gout.shape, lambda b, i, j: (0, 0)),
        ],
        out_specs=pl.BlockSpec((1, tq, GROUP_W), lambda b, i, j: (b, i, 0)),
        out_shape=jax.ShapeDtypeStruct((bsz, n_q * tq, GROUP_W), BF16),
        scratch_shapes=[q_scr, pltpu.VMEM((n_hm, tq, 1), F32), pltpu.VMEM((n_hm, tq, 128), F32),
                        pltpu.VMEM((n_hm, tq, GROUP_W), F32), pltpu.SMEM((1,), jnp.int32)],
        compiler_params=_cparams(("parallel", "parallel", "arbitrary")),
        name="attn",
    )(q, k, v, kmax, lam, gout)


def _mlstm_kernel(q_ref, k_ref, v_ref, g_ref, x_ref, tri_ref, o_ref, ct_scr, n_scr, m_scr, *, n_chunk):
    dr = pl.program_id(1)
    j = pl.program_id(2)
    lc = MLSTM_CHUNK

    @pl.when(j == 0)
    def _init():
        ct_scr[...] = jnp.zeros(ct_scr.shape, F32)
        n_scr[...] = jnp.zeros(n_scr.shape, F32)
        m_scr[...] = jnp.zeros(m_scr.shape, F32)

    row = lax.broadcasted_iota(jnp.int32, (lc, GROUP_W), 0)
    lane = lax.broadcasted_iota(jnp.int32, (lc, GROUP_W), 1)
    pos = lane % lc
    head = lane // HEAD_DIM
    sign = 1 - 2 * dr
    causal = (row - pos) * sign >= 0
    eye_t = pos == row
    blockdiag = (lax.broadcasted_iota(jnp.int32, (GROUP_W, GROUP_W), 0) // HEAD_DIM
                 == lax.broadcasted_iota(jnp.int32, (GROUP_W, GROUP_W), 1) // HEAD_DIM)
    ones_bd = jnp.where(blockdiag, 1.0, 0.0).astype(BF16)
    expand = x_ref[0]
    tri = tri_ref[0]

    ct, n_s, m_s = ct_scr[...], n_scr[...], m_scr[...]
    for jj in range(n_chunk):
        c = jnp.where(dr == 0, jj, n_chunk - 1 - jj)
        off = pl.multiple_of(c * lc, lc)
        q = q_ref[0, pl.ds(off, lc), :]
        k = k_ref[0, pl.ds(off, lc), :]
        v = v_ref[0, pl.ds(off, lc), :]
        gates = jnp.dot(g_ref[0, pl.ds(off, lc), :], expand, precision=HIGHEST, preferred_element_type=F32)
        ig, lf = gates[:, 0:GROUP_W], gates[:, GROUP_W:2 * GROUP_W]
        b = jnp.dot(tri, lf, precision=HIGHEST, preferred_element_type=F32)
        g = jnp.sum(lf, axis=0, keepdims=True)
        a = g - b + ig
        m_loc = jnp.max(a, axis=0, keepdims=True)

        r_row = jnp.sum(jnp.where(eye_t, b - ig, 0.0), axis=0, keepdims=True)
        dlog = jnp.where(causal, b - r_row, NEG)
        inter = b + m_s
        m_t = inter
        for h in range(N_HEADS):
            mh = jnp.max(jnp.where(head == h, dlog, NEG), axis=1, keepdims=True)
            m_t = jnp.where(head == h, jnp.maximum(m_t, mh), m_t)
        k_bd = jnp.where(blockdiag, jnp.concatenate([k] * N_HEADS, axis=0), jnp.zeros((), BF16))
        v_bd = jnp.where(blockdiag, jnp.concatenate([v] * N_HEADS, axis=0), jnp.zeros((), BF16))
        s = _dot_nt(q, k_bd) * jnp.exp(dlog - m_t)
        e = jnp.exp(inter - m_t)
        s_hi = s.astype(BF16)
        s_lo = (s - s_hi.astype(F32)).astype(BF16)
        num = _dot(s_hi, v_bd) + e * _dot(q, ct.astype(BF16))
        den = _dot(s_hi, ones_bd) + _dot(s_lo, ones_bd) + e * _dot_hilo(q.astype(F32) * n_s, ones_bd)
        o_ref[0, 0, pl.ds(off, lc), :] = num / jnp.maximum(jnp.abs(den), jnp.exp(-m_t))

        m_new = jnp.maximum(g + m_s, m_loc)
        a_old = jnp.exp(g + m_s - m_new)
        kw = k.astype(F32) * jnp.exp(a - m_new)
        ct = a_old * ct + jnp.where(blockdiag, _dot_tn(kw.astype(BF16), v), 0.0)
        n_s = a_old * n_s + jnp.sum(kw, axis=0, keepdims=True)
        m_s = m_new
    ct_scr[...] = ct
    n_scr[...] = n_s
    m_scr[...] = m_s


def _mlstm(lq, lk, lv, lg, expand, tri, n_lat_tok):
    bsz, tt, _ = lq.shape
    tb = ROW_TILE
    nb = tt // tb
    nbl = n_lat_tok // tb
    nbc = nb - nbl

    def blk(dr, j):
        jc = jnp.where(dr == 0, j, nbc - 1 - j) + nbl
        jl = jnp.where(dr == 0, j - nbc, nb - 1 - j)
        return jnp.where(j < nbc, jc, jl)

    def tok(width):
        return pl.BlockSpec((1, tb, width), lambda b, dr, j: (b, blk(dr, j), 0))

    return pl.pallas_call(
        functools.partial(_mlstm_kernel, n_chunk=tb // MLSTM_CHUNK),
        grid=(bsz, 2, nb),
        in_specs=[tok(GROUP_W), tok(GROUP_W), tok(GROUP_W), tok(128),
                  pl.BlockSpec((1, 128, 2 * GROUP_W), lambda b, dr, j: (dr, 0, 0)),
                  pl.BlockSpec((1, MLSTM_CHUNK, MLSTM_CHUNK), lambda b, dr, j: (dr, 0, 0))],
        out_specs=pl.BlockSpec((1, 1, tb, GROUP_W), lambda b, dr, j: (dr, b, blk(dr, j), 0)),
        out_shape=jax.ShapeDtypeStruct((2, bsz, tt, GROUP_W), F32),
        scratch_shapes=[pltpu.VMEM((GROUP_W, GROUP_W), F32), pltpu.VMEM((1, GROUP_W), F32),
                        pltpu.VMEM((1, GROUP_W), F32)],
        compiler_params=_cparams(("parallel", "parallel", "arbitrary")),
        name="mlstm",
    )(lq, lk, lv, lg, expand, tri)


def _nbr_kernel(q_ref, kp_ref, kc_ref, kn_ref, vp_ref, vc_ref, vn_ref, kx_ref, vx_ref, bias_ref, o_ref,
                kw_scr, vw_scr, *, n_rows):
    i = pl.program_id(1)
    blk = NA_ROWS * GRID_W
    kw_scr[0:blk] = kp_ref[0]
    kw_scr[blk:2 * blk] = kc_ref[0]
    kw_scr[2 * blk:3 * blk] = kn_ref[0]
    vw_scr[0:blk] = vp_ref[0]
    vw_scr[blk:2 * blk] = vc_ref[0]
    vw_scr[2 * blk:3 * blk] = vn_ref[0]
    kx = kx_ref[0]
    vx = vx_ref[0]
    lane_head = lax.broadcasted_iota(jnp.int32, (GRID_W, GROUP_W), 1) // HEAD_DIM

    def row_body(rr, carry):
        r = i * NA_ROWS + rr
        rs = jnp.clip(r - NA_ROWS // 2, 0, n_rows - NA_ROWS)
        off = pl.multiple_of((rs - (i - 1) * NA_ROWS) * GRID_W, GRID_W)
        delta = r - rs
        qoff = pl.multiple_of(rr * GRID_W, GRID_W)
        q = q_ref[0, pl.ds(qoff, GRID_W), :]
        kw = kw_scr[pl.ds(off, blk), :]
        vw = vw_scr[pl.ds(off, blk), :]
        out = jnp.zeros((GRID_W, GROUP_W), F32)
        for h in range(N_HEADS):
            qh = jnp.where(lane_head == h, q, jnp.zeros_like(q))
            sw = _dot_nt(qh, kw) + bias_ref[h, delta]
            sx = _dot_nt(qh, kx)
            m = jnp.maximum(jnp.max(sw, axis=-1, keepdims=True), jnp.max(sx, axis=-1, keepdims=True))
            pw = jnp.exp2(sw - m)
            px = jnp.exp2(sx - m)
            l = jnp.sum(pw, axis=-1, keepdims=True) + jnp.sum(px, axis=-1, keepdims=True)
            o = _dot(pw.astype(BF16), vw) + _dot(px.astype(BF16), vx)
            out = jnp.where(lane_head == h, o / l, out)
        o_ref[0, pl.ds(qoff, GRID_W), :] = out.astype(o_ref.dtype)
        return carry

    lax.fori_loop(0, NA_ROWS, row_body, 0)


def _neighbourhood(nq, nk, nv, bias, n_lat_tok, n_ctx_tok):
    bsz = nq.shape[0]
    n_rows = n_lat_tok // GRID_W
    blk = NA_ROWS * GRID_W
    nblk = n_lat_tok // blk
    ctx_blk = n_lat_tok // n_ctx_tok

    def win(shift):
        return pl.BlockSpec((1, blk, GROUP_W), lambda b, i: (b, jnp.clip(i + shift, 0, nblk - 1), 0))

    ctx_spec = pl.BlockSpec((1, n_ctx_tok, GROUP_W), lambda b, i: (b, ctx_blk, 0))
    return pl.pallas_call(
        functools.partial(_nbr_kernel, n_rows=n_rows),
        grid=(bsz, nblk),
        in_specs=[win(0), win(-1), win(0), win(1), win(-1), win(0), win(1), ctx_spec, ctx_spec,
                  pl.BlockSpec(bias.shape, lambda b, i: (0, 0, 0, 0))],
        out_specs=pl.BlockSpec((1, blk, GROUP_W), lambda b, i: (b, i, 0)),
        out_shape=jax.ShapeDtypeStruct((bsz, n_lat_tok, GROUP_W), BF16),
        scratch_shapes=[pltpu.VMEM((3 * blk, GROUP_W), BF16), pltpu.VMEM((3 * blk, GROUP_W), BF16)],
        compiler_params=_cparams(("parallel", "parallel")),
        name="nbr",
    )(nq, nk, nk, nk, nv, nv, nv, nk, nv, bias)


def _outproj_kernel(x_ref, mod_ref, yd_ref, ym_ref, hl_ref, lo_ref, yn_ref, gml_ref, g2_ref, w_ref, e64_ref,
                    x1_ref, h2_ref):
    hs = hl_ref[0, 0] + hl_ref[1, 0]
    ms = _group_sumsq(hs, e64_ref) * (1.0 / HEAD_DIM)
    yl = (hs * lax.rsqrt(ms + NORM_EPS) * gml_ref[...] * lo_ref[0].astype(F32)).astype(BF16)
    o = (_dot(yd_ref[0], w_ref[0:256, :]) + _dot(ym_ref[0], w_ref[256:512, :])
         + _dot(yl, w_ref[512:768, :]) + _dot(yn_ref[0], w_ref[768:1024, :]))
    x1 = x_ref[0] + mod_ref[0, 0, 2:3, :] * o
    x1_ref[0] = x1
    ms2 = jnp.mean(x1 * x1, axis=-1, keepdims=True)
    h2 = (x1 * lax.rsqrt(ms2 + NORM_EPS)) * g2_ref[...] * mod_ref[0, 0, 4:5, :] + mod_ref[0, 0, 3:4, :]
    h2_ref[0] = h2.astype(BF16)


def _outproj(x_all, modv, yd, ym, hl, lo, yn, gml, g2, w_out, e64, n_lat, n_tiles):
    bsz, _, d = x_all.shape
    tm = ROW_TILE

    def tok(width):
        return pl.BlockSpec((1, tm, width), lambda i, b: (b, i, 0))

    def full(a):
        return pl.BlockSpec(a.shape, lambda i, b: (0,) * a.ndim)

    return pl.pallas_call(
        _outproj_kernel,
        grid=(n_tiles, bsz),
        in_specs=[tok(d), pl.BlockSpec((1, 1, 8, d), lambda i, b: (b, (i >= n_lat).astype(jnp.int32), 0, 0)),
                  tok(GROUP_W), tok(GROUP_W),
                  pl.BlockSpec((2, 1, tm, GROUP_W), lambda i, b: (0, b, i, 0)),
                  tok(GROUP_W), tok(GROUP_W), full(gml), full(g2), full(w_out), full(e64)],
        out_specs=[tok(d), tok(d)],
        out_shape=[jax.ShapeDtypeStruct((bsz, n_tiles * tm, d), F32),
                   jax.ShapeDtypeStruct((bsz, n_tiles * tm, d), BF16)],
        compiler_params=_cparams(("parallel", "parallel")),
        name="outproj",
    )(x_all, modv, yd, ym, hl, lo, yn, gml, g2, w_out, e64)


def _ffn_kernel(x1_ref, h_ref, hp_ref, hn_ref, mod_ref, wup_ref, wc_ref, wdn_ref, o_ref, *, n_lat, n_tiles):
    i = pl.program_id(0)
    tm = h_ref.shape[1]
    h = h_ref[0]
    hp = hp_ref[0]
    hn = hn_ref[0]
    has_prev = jnp.where((i != 0) & (i != n_lat), 1.0, 0.0)
    has_next = jnp.where((i != n_lat - 1) & (i != n_tiles - 1), 1.0, 0.0)
    row = lax.broadcasted_iota(jnp.int32, (tm, FFN_CHUNK), 0)
    first, last = row == 0, row == tm - 1

    def conv_up(c0):
        w = wup_ref[:, c0:c0 + FFN_CHUNK]
        u = _dot(h, w)
        u_prev = _dot(hp, w)[FFN_HALO - 1:FFN_HALO, :] * has_prev
        u_next = _dot(hn, w)[0:1, :] * has_next
        up = jnp.where(first, u_prev, pltpu.roll(u, 1, axis=0))
        dn = jnp.where(last, u_next, pltpu.roll(u, tm - 1, axis=0))
        return (wc_ref[0:1, c0:c0 + FFN_CHUNK] * up + wc_ref[1:2, c0:c0 + FFN_CHUNK] * u
                + wc_ref[2:3, c0:c0 + FFN_CHUNK] * dn)

    acc = jnp.zeros((tm, x1_ref.shape[2]), F32)
    for c in range(FFN_HIDDEN // FFN_CHUNK):
        a = conv_up(c * FFN_CHUNK)
        g = conv_up(FFN_HIDDEN + c * FFN_CHUNK)
        act = (g * _sigmoid(g) * a).astype(BF16)
        acc = acc + _dot(act, wdn_ref[c * FFN_CHUNK:(c + 1) * FFN_CHUNK, :])
    o_ref[0] = x1_ref[0] + mod_ref[0, 0, 5:6, :] * acc


def _ffn(x1, h2, modv, w_up, w_conv, w_down, n_lat):
    bsz, rows, d = x1.shape
    tm = ROW_TILE
    n_tiles = rows // tm
    per = tm // FFN_HALO
    last = rows // FFN_HALO - 1

    def tok():
        return pl.BlockSpec((1, tm, d), lambda i, b: (b, i, 0))

    def full(a):
        return pl.BlockSpec(a.shape, lambda i, b: (0,) * a.ndim)

    return pl.pallas_call(
        functools.partial(_ffn_kernel, n_lat=n_lat, n_tiles=n_tiles),
        grid=(n_tiles, bsz),
        in_specs=[tok(), tok(),
                  pl.BlockSpec((1, FFN_HALO, d), lambda i, b: (b, jnp.maximum(i * per - 1, 0), 0)),
                  pl.BlockSpec((1, FFN_HALO, d), lambda i, b: (b, jnp.minimum((i + 1) * per, last), 0)),
                  pl.BlockSpec((1, 1, 8, d), lambda i, b: (b, (i >= n_lat).astype(jnp.int32), 0, 0)),
                  full(w_up), full(w_conv), full(w_down)],
        out_specs=tok(),
        out_shape=jax.ShapeDtypeStruct((bsz, rows, d), F32),
        compiler_params=_cparams(("parallel", "parallel")),
        name="ffn",
    )(x1, h2, h2, h2, modv, w_up, w_conv, w_down)


def _block_ones(width, group):
    idx = np.arange(width) // group
    return jnp.asarray(idx[:, None] == idx[None, :], dtype=BF16)


def _rope_table(t_len, n_ctx):
    nf = MLA_ROPE // 4
    t = jnp.arange(t_len, dtype=jnp.int32)
    row = (t // GRID_W).astype(F32)
    col = (t % GRID_W).astype(F32)
    inv = ROPE_BASE ** (-jnp.arange(nf, dtype=F32) / nf)
    ar, ac = row[:, None] * inv, col[:, None] * inv
    cr, sr, cc, sc = jnp.cos(ar), jnp.sin(ar), jnp.cos(ac), jnp.sin(ac)
    z = jnp.zeros_like(sr)
    cos = jnp.tile(jnp.concatenate([cr, cr, cc, cc], -1), (1, 4))
    sa = jnp.tile(jnp.concatenate([-sr, z, -sc, z], -1), (1, 4))
    sb = jnp.tile(jnp.concatenate([z, sr, z, sc], -1), (1, 4))
    lat = jnp.concatenate([cos, sa, sb], -1)
    ctx = jnp.concatenate([jnp.ones((n_ctx, 128), F32), jnp.zeros((n_ctx, 256), F32)], -1)
    return jnp.concatenate([lat, ctx], 0)


def _pack_w_in(w):
    d = w.shape[0]
    z = lambda n: jnp.zeros((d, n), w.dtype)
    return jnp.concatenate([w[:, 0:1152], z(64), w[:, 1152:1184], z(32), w[:, 1184:2208],
                            w[:, 2208:2224], z(112), w[:, 2224:2992]], axis=1).astype(BF16)


def _pad_heads(v, real, padded):
    lead = v.shape[:-1]
    v = v.reshape(lead + (N_HEADS, real))
    v = jnp.pad(v, [(0, 0)] * len(lead) + [(0, 0), (0, padded - real)])
    return v.reshape(lead + (N_HEADS * padded,))


def _pack_vectors(dgq, dgk, gcq, gckv, mgq, mgk, ngq, ngk, b_i, b_f, w_conv):
    def row(v):
        return jnp.pad(v.astype(F32), (0, 512 - v.shape[0]))[None]

    rows = [
        row(jnp.tile(dgq, 8) * (DIFF_DK ** -0.5 * LOG2E)), row(jnp.tile(dgk, 8)),
        row(gcq), row(gckv),
        row(_pad_heads(jnp.tile(mgq, N_HEADS), MLA_DK, MLA_PAD) * (MLA_DK ** -0.5 * LOG2E)),
        row(_pad_heads(jnp.tile(mgk, N_HEADS), MLA_DK, MLA_PAD)),
        row(jnp.tile(ngq, N_HEADS) * (HEAD_DIM ** -0.5 * LOG2E)), row(jnp.tile(ngk, N_HEADS)),
        row(jnp.concatenate([b_i.reshape(-1), b_f.reshape(-1)])),
        w_conv.astype(F32),
    ]
    rows = jnp.concatenate(rows, axis=0)
    return jnp.pad(rows, ((0, 16 - rows.shape[0]), (0, 0)))


def _nbr_bias(rpb):
    j = np.arange(NA_ROWS)
    delta = np.arange(NA_ROWS)
    roff = j[None, :] - delta[:, None] + (NA_ROWS - 1)
    cidx = np.arange(GRID_W)
    coff = np.clip(cidx[None, :] - cidx[:, None], 1 - NA_COLS, NA_COLS - 1) + (NA_COLS - 1)
    cs = np.clip(cidx - NA_COLS // 2, 0, GRID_W - NA_COLS)
    col_ok = (cidx[None, :] >= cs[:, None]) & (cidx[None, :] < cs[:, None] + NA_COLS)
    sel_r = jnp.asarray(roff[:, :, None] == np.arange(2 * NA_ROWS - 1), F32)
    sel_c = jnp.asarray(coff[:, :, None] == np.arange(2 * NA_COLS - 1), F32)
    bias = jnp.einsum('hrc,djr,qkc->hdqjk', rpb.astype(F32) * LOG2E, sel_r, sel_c, precision=HIGHEST)
    bias = bias + jnp.asarray(np.where(col_ok, 0.0, NEG), F32)[None, None, :, None, :]
    return bias.reshape(N_HEADS, NA_ROWS, GRID_W, NA_ROWS * GRID_W)


def _mlstm_consts():
    expand = np.zeros((2, 128, 2 * GROUP_W), np.float32)
    for dr in range(2):
        for h in range(N_HEADS):
            expand[dr, dr * N_HEADS + h, h * HEAD_DIM:(h + 1) * HEAD_DIM] = 1.0
            expand[dr, 2 * N_HEADS + dr * N_HEADS + h, GROUP_W + h * HEAD_DIM:GROUP_W + (h + 1) * HEAD_DIM] = 1.0
    lower = np.tril(np.ones((MLSTM_CHUNK, MLSTM_CHUNK), np.float32))
    return jnp.asarray(expand), jnp.asarray(np.stack([lower, lower.T]))


def _kv_tile(total):
    best = 128
    for t in range(128, 1025, 128):
        if total % t == 0:
            best = t
    return best


def kernel(x, c, ctx, c_ctx, w_mod, b_mod, g_norm1, g_norm2, w_in, w_out, diff_g_q, diff_g_k, diff_lam, diff_g_out, mla_g_cq, mla_g_ckv, mla_w_uq, mla_w_ukv, mla_g_q, mla_g_k, mlstm_w_conv, mlstm_b_i, mlstm_b_f, mlstm_g_out, na_g_q, na_g_k, na_rpb, ffn_w_up, ffn_w_conv, ffn_w_down):
    bsz, t_lat, d = x.shape
    n_ctx = ctx.shape[1]
    depth = w_in.shape[0]
    tt = t_lat + n_ctx
    tm = ROW_TILE
    assert d == D_MODEL and t_lat % (NA_ROWS * GRID_W) == 0 and n_ctx % tm == 0 and bsz < 8
    n_lat, n_tiles = t_lat // tm, tt // tm

    c_rows = jnp.zeros((8, d), F32).at[:bsz].set(c).at[bsz].set(c_ctx)
    mod_all = _modulation(c_rows, w_mod, b_mod)
    tab = _rope_table(t_lat, n_ctx)
    consts = (_block_ones(256, DIFF_DK), _block_ones(256, HEAD_DIM), _block_ones(256, MLA_PAD))
    expand, tri = _mlstm_consts()
    tk_lat = _kv_tile(tt)

    x_all = jnp.concatenate([x, ctx], axis=1)
    for l in range(depth):
        need_ctx = l < depth - 1
        lam_init = 0.8 - 0.6 * math.exp(-0.3 * l)
        m = mod_all[l].reshape(8, 6, d)
        lat = jnp.pad(m[:bsz], ((0, 0), (0, 2), (0, 0)))
        cx = jnp.broadcast_to(jnp.pad(m[bsz], ((0, 2), (0, 0)))[None], lat.shape)
        modv = jnp.stack([lat, cx], axis=1)
        modv = modv.at[:, :, 1].add(1.0).at[:, :, 4].add(1.0)

        wukv = mla_w_ukv[l].reshape(MLA_KV_LORA, N_HEADS, MLA_NOPE + HEAD_DIM)
        wuk = jnp.pad(wukv[:, :, :MLA_NOPE], ((0, 0), (0, 0), (0, MLA_PAD - MLA_NOPE))).reshape(MLA_KV_LORA, -1)
        wuv = wukv[:, :, MLA_NOPE:].reshape(MLA_KV_LORA, -1)
        vecs = _pack_vectors(diff_g_q[l], diff_g_k[l], mla_g_cq[l], mla_g_ckv[l], mla_g_q[l], mla_g_k[l],
                             na_g_q[l], na_g_k[l], mlstm_b_i[l], mlstm_b_f[l], mlstm_w_conv[l])
        (dq, dkt, dv, mq, mkt, mv, lq, lk, lv, lo, lg, nq, nk, nv, dkn, mkn) = _inproj(
            x_all, modv, g_norm1[l][None], _pack_w_in(w_in[l]), tab, consts, vecs,
            _pad_heads(mla_w_uq[l], MLA_DK, MLA_PAD).astype(BF16), wuk.astype(BF16), wuv.astype(BF16), n_lat)
        dkmax = jnp.max(dkn[:, :, 0, :], axis=1)[:, None, :]
        mkmax = jnp.max(mkn[:, :, 0, :], axis=1)[:, None, :]

        lam = diff_lam[l].astype(F32)
        gout = jnp.tile(diff_g_out[l].astype(F32), N_HEADS)[None]
        lat_q = dict(q_off=0, n_q=t_lat // ATTN_TQ, tq=ATTN_TQ, k_off=0, n_k=tt // tk_lat, tk=tk_lat)
        ctx_q = dict(q_off=n_lat, n_q=n_ctx // tm, tq=tm, k_off=t_lat // n_ctx, n_k=1, tk=n_ctx)
        diff_cfg = dict(n_maps=2, qs=DIFF_DK, ks=DIFF_KS, aug_lane=DIFF_DK, transposed=True, lam_init=lam_init)
        mla_cfg = dict(n_maps=1, qs=MLA_PAD, ks=MLA_PAD, aug_lane=MLA_DK, transposed=True, lam_init=0.0)
        yd = _attention(dq, dkt, dv, dkmax, lam, gout, **lat_q, **diff_cfg)
        ym = _attention(mq, mkt, mv, mkmax, lam, gout, **lat_q, **mla_cfg)
        yn = _neighbourhood(nq, nk, nv, _nbr_bias(na_rpb[l]), t_lat, n_ctx)
        hl = _mlstm(lq, lk, lv, lg, expand, tri, t_lat)
        if need_ctx:
            na_cfg = dict(n_maps=1, qs=HEAD_DIM, ks=HEAD_DIM, aug_lane=0, transposed=False, lam_init=0.0)
            yd = jnp.concatenate([yd, _attention(dq, dkt, dv, dkmax, lam, gout, **ctx_q, **diff_cfg)], axis=1)
            ym = jnp.concatenate([ym, _attention(mq, mkt, mv, mkmax, lam, gout, **ctx_q, **mla_cfg)], axis=1)
            yn = jnp.concatenate([yn, _attention(nq, nk, nv, dkmax, lam, gout, **ctx_q, **na_cfg)], axis=1)
        rows = n_tiles if need_ctx else n_lat
        x1, h2 = _outproj(x_all, modv, yd, ym, hl, lo, yn, jnp.tile(mlstm_g_out[l].astype(F32), N_HEADS)[None],
                          g_norm2[l][None], w_out[l].astype(BF16), consts[1], n_lat, rows)
        x_all = _ffn(x1, h2, modv, ffn_w_up[l].astype(BF16), ffn_w_conv[l].astype(F32),
                     ffn_w_down[l].astype(BF16), n_lat)
    return x_all[:, :t_lat]
```

```python
import functools
import math

import numpy as np
import jax
import jax.numpy as jnp
from jax import lax
from jax.experimental import pallas as pl
from jax.experimental.pallas import tpu as pltpu

F32 = jnp.float32
BF16 = jnp.bfloat16
HIGHEST = lax.Precision.HIGHEST

D_MODEL = 1024
GRID_W = 64
N_HEADS = 4
HEAD_DIM = 64
GROUP_W = N_HEADS * HEAD_DIM
DIFF_DK = 32
MLA_Q_LORA = 256
MLA_KV_LORA = 128
MLA_NOPE = 64
MLA_ROPE = 32
MLA_DK = MLA_NOPE + MLA_ROPE
MLA_PAD = 128
DIFF_KS = 64
BOUND_LIMIT = 50.0
MLSTM_CHUNK = 64
NA_ROWS = 8
NA_COLS = 16
FFN_HIDDEN = 2816
ROPE_BASE = 10000.0
NORM_EPS = 1e-6
LOG2E = 1.4426950408889634
NEG = -1e30

ROW_TILE = 256
ATTN_TQ = 512
FFN_CHUNK = 256
FFN_HALO = 16
VMEM_LIMIT = 56 * 1024 * 1024

C_DQ, C_DK, C_DV, C_CQ, C_CKV, C_KR = 0, 256, 512, 768, 1024, 1152
C_LQ, C_LV, C_LO, C_LG, C_NQ, C_NK, C_NV = 1280, 1792, 2048, 2304, 2432, 2688, 2944
W1_COLS = 3200
V_DGQ, V_DGK, V_GCQ, V_GCKV, V_MGQ, V_MGK, V_NGQ, V_NGK, V_GB, V_CW = 0, 1, 2, 3, 4, 5, 6, 7, 8, 9


def _cparams(sem):
    return pltpu.CompilerParams(dimension_semantics=sem, vmem_limit_bytes=VMEM_LIMIT)


def _dot(a, b):
    return jnp.dot(a, b, preferred_element_type=F32)


def _dot_nt(a, b):
    return lax.dot_general(a, b, (((1,), (1,)), ((), ())), preferred_element_type=F32)


def _dot_tn(a, b):
    return lax.dot_general(a, b, (((0,), (0,)), ((), ())), preferred_element_type=F32)


def _dot_hilo(x, w):
    hi = x.astype(BF16)
    lo = (x - hi.astype(F32)).astype(BF16)
    return _dot(hi, w) + _dot(lo, w)


def _sigmoid(x):
    return 1.0 / (1.0 + jnp.exp(-x))


def _group_sumsq(x, e_ref):
    w = x.shape[1]
    sq = (x * x).astype(BF16)
    return jnp.concatenate([_dot(sq[:, c:c + 256], e_ref[...]) for c in range(0, w, 256)], axis=1)


def _mod_kernel(c_ref, w_ref, b_ref, o_ref):
    c = c_ref[...]
    a = c * _sigmoid(c)
    o_ref[0] = _dot(a.astype(BF16), w_ref[0].astype(BF16)) + b_ref[0]


def _modulation(c_rows, w_mod, b_mod):
    n_layers, d, n_out = w_mod.shape
    tn = 1536
    return pl.pallas_call(
        _mod_kernel,
        grid=(n_layers, n_out // tn),
        in_specs=[pl.BlockSpec((8, d), lambda l, j: (0, 0)),
                  pl.BlockSpec((1, d, tn), lambda l, j: (l, 0, j)),
                  pl.BlockSpec((1, 1, tn), lambda l, j: (l, 0, j))],
        out_specs=pl.BlockSpec((1, 8, tn), lambda l, j: (l, 0, j)),
        out_shape=jax.ShapeDtypeStruct((n_layers, 8, n_out), F32),
        compiler_params=_cparams(("parallel", "parallel")),
        name="mod",
    )(c_rows, w_mod, b_mod.reshape(n_layers, 1, n_out))


def _inproj_kernel(x_ref, xp_ref, xn_ref, mod_ref, g1_ref, w_ref, tab_ref, e32_ref, e64_ref, e128_ref,
                   vec_ref, wuq_ref, wuk_ref, wuv_ref,
                   dq_ref, dkt_ref, dv_ref, mq_ref, mkt_ref, mv_ref,
                   lq_ref, lk_ref, lv_ref, lo_ref, lg_ref, nq_ref, nk_ref, nv_ref, dkn_ref, mkn_ref,
                   *, n_lat, n_tiles):
    i = pl.program_id(0)
    tm = x_ref.shape[1]
    g1 = g1_ref[...]
    shift = mod_ref[0, 0, 0:1, :]
    scale = mod_ref[0, 0, 1:2, :]

    def normmod(xv):
        ms = jnp.mean(xv * xv, axis=-1, keepdims=True)
        return ((xv * lax.rsqrt(ms + NORM_EPS)) * g1 * scale + shift).astype(BF16)

    h = normmod(x_ref[0])

    def proj(c0, width):
        return _dot(h, w_ref[:, c0:c0 + width])

    def vec(r, width):
        return vec_ref[r:r + 1, 0:width]

    tab = tab_ref[...]
    cos128, sa128, sb128 = tab[:, 0:128], tab[:, 128:256], tab[:, 256:384]

    def rope(t, cos, sa, sb):
        w = t.shape[1]
        return t * cos + pltpu.roll(t, w - 8, axis=1) * sa + pltpu.roll(t, 8, axis=1) * sb

    def tile_lanes(t, n):
        return jnp.concatenate([t] * n, axis=1)

    cos_d, sa_d, sb_d = tile_lanes(cos128, 2), tile_lanes(sa128, 2), tile_lanes(sb128, 2)
    xq = proj(C_DQ, 256)
    qn = xq * lax.rsqrt(_group_sumsq(xq, e32_ref) * (1.0 / DIFF_DK) + NORM_EPS) * vec(V_DGQ, 256)
    dq_ref[0] = rope(qn, cos_d, sa_d, sb_d).astype(BF16)
    xk = proj(C_DK, 256)
    kn = xk * lax.rsqrt(_group_sumsq(xk, e32_ref) * (1.0 / DIFF_DK) + NORM_EPS) * vec(V_DGK, 256)
    dkn_ref[0, 0] = jnp.broadcast_to(jnp.max(_group_sumsq(kn, e32_ref), axis=0, keepdims=True), (8, 256))
    kt = rope(kn, cos_d, sa_d, sb_d).T
    ones_row = jnp.where(lax.broadcasted_iota(jnp.int32, (DIFF_KS - DIFF_DK, tm), 0) == 0, 1.0, 0.0)
    dkt_ref[0] = jnp.concatenate(
        [piece for mp in range(2 * N_HEADS) for piece in (kt[mp * DIFF_DK:(mp + 1) * DIFF_DK], ones_row)],
        axis=0).astype(BF16)
    dv_ref[0] = proj(C_DV, 256).astype(BF16)

    lane128 = lax.broadcasted_iota(jnp.int32, (tm, 128), 1)
    in_rope = (lane128 >= MLA_NOPE) & (lane128 < MLA_DK)
    cos_m = tile_lanes(jnp.where(in_rope, cos128, 1.0), 4)
    sa_m = tile_lanes(jnp.where(in_rope, sa128, 0.0), 4)
    sb_m = tile_lanes(jnp.where(in_rope, sb128, 0.0), 4)
    cq = proj(C_CQ, 256)
    cqn = cq * lax.rsqrt(jnp.mean(cq * cq, axis=-1, keepdims=True) + NORM_EPS) * vec(V_GCQ, 256)
    q = _dot(cqn.astype(BF16), wuq_ref[...])
    q = q * lax.rsqrt(_group_sumsq(q, e128_ref) * (1.0 / MLA_DK) + NORM_EPS) * vec(V_MGQ, 512)
    mq_ref[0] = rope(q, cos_m, sa_m, sb_m).astype(BF16)
    ckv = proj(C_CKV, 128)
    ckvn = (ckv * lax.rsqrt(jnp.mean(ckv * ckv, axis=-1, keepdims=True) + NORM_EPS) * vec(V_GCKV, 128)).astype(BF16)
    k = _dot(ckvn, wuk_ref[...]) + tile_lanes(proj(C_KR, 128), 4)
    k = k * lax.rsqrt(_group_sumsq(k, e128_ref) * (1.0 / MLA_DK) + NORM_EPS) * vec(V_MGK, 512)
    mkn_ref[0, 0] = jnp.broadcast_to(jnp.max(_group_sumsq(k, e128_ref), axis=0, keepdims=True), (8, 512))
    k = jnp.where(tile_lanes(lane128 == MLA_DK, 4), 1.0, rope(k, cos_m, sa_m, sb_m))
    mkt_ref[0] = k.T.astype(BF16)
    mv_ref[0] = _dot(ckvn, wuv_ref[...]).astype(BF16)

    u = proj(C_LQ, 512)
    w_lqk = w_ref[:, C_LQ:C_LQ + 512]
    u_prev = _dot(normmod(xp_ref[0]), w_lqk)[7:8, :]
    u_next = _dot(normmod(xn_ref[0]), w_lqk)[0:1, :]
    has_prev = jnp.where((i != 0) & (i != n_lat), 1.0, 0.0)
    has_next = jnp.where((i != n_lat - 1) & (i != n_tiles - 1), 1.0, 0.0)
    row = lax.broadcasted_iota(jnp.int32, (tm, 512), 0)
    up = jnp.where(row == 0, u_prev * has_prev, pltpu.roll(u, 1, axis=0))
    dn = jnp.where(row == tm - 1, u_next * has_next, pltpu.roll(u, tm - 1, axis=0))
    conv = vec(V_CW, 512) * up + vec(V_CW + 1, 512) * u + vec(V_CW + 2, 512) * dn
    qk = conv * _sigmoid(conv)
    lq_ref[0] = qk[:, 0:256].astype(BF16)
    lk_ref[0] = (qk[:, 256:512] * (HEAD_DIM ** -0.5)).astype(BF16)
    lv_ref[0] = proj(C_LV, 256).astype(BF16)
    lo_ref[0] = _sigmoid(proj(C_LO, 256)).astype(BF16)
    gt = proj(C_LG, 128) + vec(V_GB, 128)
    log_sig = jnp.minimum(gt, 0.0) - jnp.log1p(jnp.exp(-jnp.abs(gt)))
    lg_ref[0] = jnp.where(lane128 < 2 * N_HEADS, gt, log_sig)

    xq = proj(C_NQ, 256)
    nq_ref[0] = (xq * lax.rsqrt(_group_sumsq(xq, e64_ref) * (1.0 / HEAD_DIM) + NORM_EPS) * vec(V_NGQ, 256)).astype(BF16)
    xk = proj(C_NK, 256)
    nk_ref[0] = (xk * lax.rsqrt(_group_sumsq(xk, e64_ref) * (1.0 / HEAD_DIM) + NORM_EPS) * vec(V_NGK, 256)).astype(BF16)
    nv_ref[0] = proj(C_NV, 256).astype(BF16)


def _inproj(x_all, modv, g1, w1, tab, consts, vecs, wuq, wuk, wuv, n_lat):
    bsz, tt, d = x_all.shape
    tm = ROW_TILE
    n_tiles = tt // tm
    last8 = tt // 8 - 1

    def full(a):
        return pl.BlockSpec(a.shape, lambda i, b: (0,) * a.ndim)

    def tok(width):
        return pl.BlockSpec((1, tm, width), lambda i, b: (b, i, 0))

    def tok_t(rows):
        return pl.BlockSpec((1, rows, tm), lambda i, b: (b, 0, i))

    def sds(shape, dt=BF16):
        return jax.ShapeDtypeStruct(shape, dt)

    e32, e64, e128 = consts
    outs = [
        (sds((bsz, tt, 256)), tok(256)), (sds((bsz, 8 * DIFF_KS, tt)), tok_t(8 * DIFF_KS)), (sds((bsz, tt, 256)), tok(256)),
        (sds((bsz, tt, 512)), tok(512)), (sds((bsz, 512, tt)), tok_t(512)), (sds((bsz, tt, 256)), tok(256)),
        (sds((bsz, tt, 256)), tok(256)), (sds((bsz, tt, 256)), tok(256)), (sds((bsz, tt, 256)), tok(256)),
        (sds((bsz, tt, 256)), tok(256)), (sds((bsz, tt, 128), F32), tok(128)),
        (sds((bsz, tt, 256)), tok(256)), (sds((bsz, tt, 256)), tok(256)), (sds((bsz, tt, 256)), tok(256)),
        (sds((bsz, n_tiles, 8, 256), F32), pl.BlockSpec((1, 1, 8, 256), lambda i, b: (b, i, 0, 0))),
        (sds((bsz, n_tiles, 8, 512), F32), pl.BlockSpec((1, 1, 8, 512), lambda i, b: (b, i, 0, 0))),
    ]
    return pl.pallas_call(
        functools.partial(_inproj_kernel, n_lat=n_lat, n_tiles=n_tiles),
        grid=(n_tiles, bsz),
        in_specs=[
            pl.BlockSpec((1, tm, d), lambda i, b: (b, i, 0)),
            pl.BlockSpec((1, 8, d), lambda i, b: (b, jnp.maximum(i * (tm // 8) - 1, 0), 0)),
            pl.BlockSpec((1, 8, d), lambda i, b: (b, jnp.minimum((i + 1) * (tm // 8), last8), 0)),
            pl.BlockSpec((1, 1, 8, d), lambda i, b: (b, (i >= n_lat).astype(jnp.int32), 0, 0)),
            full(g1), full(w1),
            pl.BlockSpec((tm, 384), lambda i, b: (i, 0)),
            full(e32), full(e64), full(e128), full(vecs), full(wuq), full(wuk), full(wuv),
        ],
        out_specs=[o[1] for o in outs],
        out_shape=[o[0] for o in outs],
        compiler_params=_cparams(("parallel", "parallel")),
        name="inproj",
    )(x_all, x_all, x_all, modv, g1, w1, tab, e32, e64, e128, vecs, wuq, wuk, wuv)


def _attn_kernel(q_ref, k_ref, v_ref, kmax_ref, lam_ref, gout_ref, o_ref, q_scr, m_scr, l_scr, acc_scr, flag_scr, *,
                 n_maps, qs, ks, aug_lane, transposed, lam_init):
    ki = pl.program_id(2)
    n_hm = N_HEADS * n_maps
    tq = q_ref.shape[1]
    tk = v_ref.shape[1]
    lane_head = lax.broadcasted_iota(jnp.int32, (tq, GROUP_W), 1) // HEAD_DIM

    @pl.when(ki == 0)
    def _init():
        if transposed:
            lane = lax.broadcasted_iota(jnp.int32, (tq, ks), 1)
            worst = jnp.zeros((tq, 1), F32)
            for hm in range(n_hm):
                qh = q_ref[0, :, hm * qs:(hm + 1) * qs].astype(F32)
                bound = jnp.sqrt(jnp.sum(qh * qh, axis=-1, keepdims=True)
                                 * kmax_ref[0, :, hm * qs:hm * qs + 1]) * 1.02 + 0.01
                if ks > qs:
                    qh = jnp.concatenate([qh, jnp.zeros((tq, ks - qs), F32)], axis=1)
                q_scr[hm] = jnp.where(lane == aug_lane, -bound, qh).astype(BF16)
                worst = jnp.maximum(worst, bound)
            flag_scr[0] = (jnp.max(worst) > BOUND_LIMIT).astype(jnp.int32)
        else:
            for hm in range(n_hm):
                q_scr[hm] = jnp.where(lane_head == hm, q_ref[0], jnp.zeros_like(q_ref[0]))
            flag_scr[0] = 1
        m_scr[...] = jnp.full(m_scr.shape, NEG, F32)
        l_scr[...] = jnp.zeros(l_scr.shape, F32)
        acc_scr[...] = jnp.zeros(acc_scr.shape, F32)

    def scores(hm):
        if transposed:
            return _dot(q_scr[hm], k_ref[0, hm * ks:(hm + 1) * ks, :])
        return _dot_nt(q_scr[hm], k_ref[0])

    def lane_partial_sums(p):
        return functools.reduce(lambda a, b: a + b, [p[:, c:c + 128] for c in range(0, tk, 128)])

    def single_pass():
        v = v_ref[0]
        for hm in range(n_hm):
            p = jnp.exp2(scores(hm))
            l_scr[hm] += lane_partial_sums(p)
            acc_scr[hm] += _dot(p.astype(BF16), v)

    def online():
        v = v_ref[0]
        for hm in range(n_hm):
            s = scores(hm)
            m_prev = m_scr[hm]
            m_new = jnp.maximum(m_prev, jnp.max(s, axis=-1, keepdims=True))
            alpha = jnp.exp2(m_prev - m_new)
            p = jnp.exp2(s - m_new)
            l_scr[hm] = alpha * l_scr[hm] + lane_partial_sums(p)
            acc_scr[hm] = alpha * acc_scr[hm] + _dot(p.astype(BF16), v)
            m_scr[hm] = m_new

    if transposed:
        pl.when(flag_scr[0] == 0)(single_pass)
        pl.when(flag_scr[0] != 0)(online)
    else:
        online()

    @pl.when(ki == pl.num_programs(2) - 1)
    def _finish():
        out = jnp.zeros((tq, GROUP_W), F32)
        if n_maps == 2:
            lv = lam_ref[...]
            lam = (jnp.exp(jnp.sum(lv[0:1] * lv[1:2], axis=-1, keepdims=True))
                   - jnp.exp(jnp.sum(lv[2:3] * lv[3:4], axis=-1, keepdims=True)) + lam_init)

        def normalised(hm):
            return acc_scr[hm] / jnp.sum(l_scr[hm], axis=-1, keepdims=True)

        for h in range(N_HEADS):
            if n_maps == 2:
                o = normalised(2 * h) - lam * normalised(2 * h + 1)
                ms = jnp.sum(jnp.where(lane_head == h, o * o, 0.0), axis=-1, keepdims=True) * (1.0 / HEAD_DIM)
                o = o * lax.rsqrt(ms + NORM_EPS) * gout_ref[...] * (1.0 - lam_init)
            else:
                o = normalised(h)
            out = jnp.where(lane_head == h, o, out)
        o_ref[0] = out.astype(o_ref.dtype)


def _attention(q, k, v, kmax, lam, gout, *, q_off, n_q, tq, k_off, n_k, tk, n_maps, qs, ks, aug_lane, transposed,
               lam_init):
    bsz = q.shape[0]
    n_hm = N_HEADS * n_maps
    if transposed:
        k_spec = pl.BlockSpec((1, k.shape[1], tk), lambda b, i, j: (b, 0, j + k_off))
        q_scr = pltpu.VMEM((n_hm, tq, ks), BF16)
    else:
        k_spec = pl.BlockSpec((1, tk, k.shape[2]), lambda b, i, j: (b, j + k_off, 0))
        q_scr = pltpu.VMEM((n_hm, tq, GROUP_W), BF16)
    return pl.pallas_call(
        functools.partial(_attn_kernel, n_maps=n_maps, qs=qs, ks=ks, aug_lane=aug_lane, transposed=transposed,
                          lam_init=lam_init),
        grid=(bsz, n_q, n_k),
        in_specs=[
            pl.BlockSpec((1, tq, q.shape[2]), lambda b, i, j: (b, i + q_off, 0)),
            k_spec,
            pl.BlockSpec((1, tk, GROUP_W), lambda b, i, j: (b, j + k_off, 0)),
            pl.BlockSpec((1, 1, kmax.shape[2]), lambda b, i, j: (b, 0, 0)),
            pl.BlockSpec(lam.shape, lambda b, i, j: (0, 0)),
            pl.BlockSpec(gout.shape, lambda b, i, j: (0, 0)),
        ],
        out_specs=pl.BlockSpec((1, tq, GROUP_W), lambda b, i, j: (b, i, 0)),
        out_shape=jax.ShapeDtypeStruct((bsz, n_q * tq, GROUP_W), BF16),
        scratch_shapes=[q_scr, pltpu.VMEM((n_hm, tq, 1), F32), pltpu.VMEM((n_hm, tq, 128), F32),
                        pltpu.VMEM((n_hm, tq, GROUP_W), F32), pltpu.SMEM((1,), jnp.int32)],
        compiler_params=_cparams(("parallel", "parallel", "arbitrary")),
        name="attn",
    )(q, k, v, kmax, lam, gout)


def _dot_split3(x, w, left):
    hi = x.astype(BF16)
    r1 = x - hi.astype(F32)
    mid = r1.astype(BF16)
    lo = (r1 - mid.astype(F32)).astype(BF16)
    if left:
        return _dot(w, hi) + _dot(w, mid) + _dot(w, lo)
    return _dot(hi, w) + _dot(mid, w) + _dot(lo, w)


def _mlstm_kernel(qf_ref, kf_ref, vf_ref, gf_ref, qb_ref, kb_ref, vb_ref, gb_ref, x_ref, tri_ref,
                  of_ref, ob_ref, ct_scr, n_scr, m_scr, *, n_chunk):
    j = pl.program_id(0)
    lc = MLSTM_CHUNK
    bsz = qf_ref.shape[0]

    @pl.when(j == 0)
    def _init():
        ct_scr[...] = jnp.zeros(ct_scr.shape, F32)
        n_scr[...] = jnp.zeros(n_scr.shape, F32)
        m_scr[...] = jnp.zeros(m_scr.shape, F32)

    row = lax.broadcasted_iota(jnp.int32, (lc, GROUP_W), 0)
    lane = lax.broadcasted_iota(jnp.int32, (lc, GROUP_W), 1)
    pos = lane % lc
    head = lane // HEAD_DIM
    eye_t = pos == row
    blockdiag = (lax.broadcasted_iota(jnp.int32, (GROUP_W, GROUP_W), 0) // HEAD_DIM
                 == lax.broadcasted_iota(jnp.int32, (GROUP_W, GROUP_W), 1) // HEAD_DIM)
    ones_bd = jnp.where(blockdiag, 1.0, 0.0).astype(BF16)

    for dr, (q_ref, k_ref, v_ref, g_ref, o_ref) in enumerate(
            ((qf_ref, kf_ref, vf_ref, gf_ref, of_ref), (qb_ref, kb_ref, vb_ref, gb_ref, ob_ref))):
        causal = (pos <= row) if dr == 0 else (pos >= row)
        expand = x_ref[dr]
        tri = tri_ref[dr]
        for bi in range(bsz):
            ct, n_s, m_s = ct_scr[dr, bi], n_scr[dr, bi], m_scr[dr, bi]
            gates = _dot_split3(g_ref[bi], expand, left=False)
            ig_all, lf_all = gates[:, 0:GROUP_W], gates[:, GROUP_W:2 * GROUP_W]
            b_all = _dot_split3(lf_all, tri, left=True)
            for jj in range(n_chunk):
                c = jj if dr == 0 else n_chunk - 1 - jj
                sl = slice(c * lc, (c + 1) * lc)
                q, k, v = q_ref[bi, sl, :], k_ref[bi, sl, :], v_ref[bi, sl, :]
                ig, lf, b = ig_all[sl], lf_all[sl], b_all[sl]
                g = jnp.sum(lf, axis=0, keepdims=True)
                a = g - b + ig
                m_loc = jnp.max(a, axis=0, keepdims=True)

                r_row = jnp.sum(jnp.where(eye_t, b - ig, 0.0), axis=0, keepdims=True)
                dlog = jnp.where(causal, b - r_row, NEG)
                inter = b + m_s
                m_t = inter
                for h in range(N_HEADS):
                    mh = jnp.max(jnp.where(head == h, dlog, NEG), axis=1, keepdims=True)
                    m_t = jnp.where(head == h, jnp.maximum(m_t, mh), m_t)
                k_bd = jnp.where(blockdiag, jnp.concatenate([k] * N_HEADS, axis=0), jnp.zeros((), BF16))
                v_bd = jnp.where(blockdiag, jnp.concatenate([v] * N_HEADS, axis=0), jnp.zeros((), BF16))
                s = _dot_nt(q, k_bd) * jnp.exp(dlog - m_t)
                e = jnp.exp(inter - m_t)
                s_hi = s.astype(BF16)
                s_lo = (s - s_hi.astype(F32)).astype(BF16)
                num = _dot(s_hi, v_bd) + e * _dot(q, ct.astype(BF16))
                den = _dot(s_hi, ones_bd) + _dot(s_lo, ones_bd) + e * _dot_hilo(q.astype(F32) * n_s, ones_bd)
                o_ref[bi, sl, :] = num / jnp.maximum(jnp.abs(den), jnp.exp(-m_t))

                m_new = jnp.maximum(g + m_s, m_loc)
                a_old = jnp.exp(g + m_s - m_new)
                kw = k.astype(F32) * jnp.exp(a - m_new)
                ct = a_old * ct + jnp.where(blockdiag, _dot_tn(kw.astype(BF16), v), 0.0)
                n_s = a_old * n_s + jnp.sum(kw, axis=0, keepdims=True)
                m_s = m_new
            ct_scr[dr, bi] = ct
            n_scr[dr, bi] = n_s
            m_scr[dr, bi] = m_s


def _mlstm(lq, lk, lv, lg, expand, tri, n_lat_tok):
    bsz, tt, _ = lq.shape
    tb = ROW_TILE
    nb = tt // tb
    nbl = n_lat_tok // tb
    nbc = nb - nbl

    def fwd(j):
        return jnp.where(j < nbc, j + nbl, j - nbc)

    def bwd(j):
        return jnp.where(j < nbc, nbl + nbc - 1 - j, nb - 1 - j)

    def tok(width, order):
        return pl.BlockSpec((bsz, tb, width), lambda j: (0, order(j), 0))

    def full(a):
        return pl.BlockSpec(a.shape, lambda j: (0,) * a.ndim)

    out = jax.ShapeDtypeStruct((bsz, tt, GROUP_W), F32)
    return pl.pallas_call(
        functools.partial(_mlstm_kernel, n_chunk=tb // MLSTM_CHUNK),
        grid=(nb,),
        in_specs=[tok(GROUP_W, fwd), tok(GROUP_W, fwd), tok(GROUP_W, fwd), tok(128, fwd),
                  tok(GROUP_W, bwd), tok(GROUP_W, bwd), tok(GROUP_W, bwd), tok(128, bwd),
                  full(expand), full(tri)],
        out_specs=[tok(GROUP_W, fwd), tok(GROUP_W, bwd)],
        out_shape=[out, out],
        scratch_shapes=[pltpu.VMEM((2, bsz, GROUP_W, GROUP_W), F32), pltpu.VMEM((2, bsz, 1, GROUP_W), F32),
                        pltpu.VMEM((2, bsz, 1, GROUP_W), F32)],
        compiler_params=_cparams(("arbitrary",)),
        name="mlstm",
    )(lq, lk, lv, lg, lq, lk, lv, lg, expand, tri)


def _nbr_kernel(q_ref, kp_ref, kc_ref, kn_ref, vp_ref, vc_ref, vn_ref, kx_ref, vx_ref, bias_ref, o_ref,
                kw_scr, vw_scr, *, n_rows):
    i = pl.program_id(1)
    blk = NA_ROWS * GRID_W
    kw_scr[0:blk] = kp_ref[0]
    kw_scr[blk:2 * blk] = kc_ref[0]
    kw_scr[2 * blk:3 * blk] = kn_ref[0]
    vw_scr[0:blk] = vp_ref[0]
    vw_scr[blk:2 * blk] = vc_ref[0]
    vw_scr[2 * blk:3 * blk] = vn_ref[0]
    kx = kx_ref[0]
    vx = vx_ref[0]
    lane_head = lax.broadcasted_iota(jnp.int32, (GRID_W, GROUP_W), 1) // HEAD_DIM
    blockdiag = (lax.broadcasted_iota(jnp.int32, (GROUP_W, GROUP_W), 0) // GRID_W
                 == lax.broadcasted_iota(jnp.int32, (GROUP_W, GROUP_W), 1) // HEAD_DIM)

    for rr in range(NA_ROWS):
        r = i * NA_ROWS + rr
        rs = jnp.clip(r - NA_ROWS // 2, 0, n_rows - NA_ROWS)
        off = pl.multiple_of((rs - (i - 1) * NA_ROWS) * GRID_W, GRID_W)
        delta = r - rs
        q = q_ref[0, rr * GRID_W:(rr + 1) * GRID_W, :]
        kw = kw_scr[pl.ds(off, blk), :]
        vw = vw_scr[pl.ds(off, blk), :]
        q_bd = jnp.where(blockdiag, jnp.concatenate([q] * N_HEADS, axis=0), jnp.zeros((), BF16))
        sw = _dot_nt(q_bd, kw) + bias_ref[delta]
        sx = _dot_nt(q_bd, kx)
        m = jnp.maximum(jnp.max(sw, axis=-1, keepdims=True), jnp.max(sx, axis=-1, keepdims=True))
        pw = jnp.exp2(sw - m)
        px = jnp.exp2(sx - m)
        l = jnp.sum(pw, axis=-1, keepdims=True) + jnp.sum(px, axis=-1, keepdims=True)
        o = (_dot(pw.astype(BF16), vw) + _dot(px.astype(BF16), vx)) / l
        out = jnp.zeros((GRID_W, GROUP_W), F32)
        for h in range(N_HEADS):
            out = jnp.where(lane_head == h, o[h * GRID_W:(h + 1) * GRID_W], out)
        o_ref[0, rr * GRID_W:(rr + 1) * GRID_W, :] = out.astype(o_ref.dtype)


def _neighbourhood(nq, nk, nv, bias, n_lat_tok, n_ctx_tok):
    bsz = nq.shape[0]
    n_rows = n_lat_tok // GRID_W
    blk = NA_ROWS * GRID_W
    nblk = n_lat_tok // blk
    ctx_blk = n_lat_tok // n_ctx_tok

    def win(shift):
        return pl.BlockSpec((1, blk, GROUP_W), lambda b, i: (b, jnp.clip(i + shift, 0, nblk - 1), 0))

    ctx_spec = pl.BlockSpec((1, n_ctx_tok, GROUP_W), lambda b, i: (b, ctx_blk, 0))
    return pl.pallas_call(
        functools.partial(_nbr_kernel, n_rows=n_rows),
        grid=(bsz, nblk),
        in_specs=[win(0), win(-1), win(0), win(1), win(-1), win(0), win(1), ctx_spec, ctx_spec,
                  pl.BlockSpec(bias.shape, lambda b, i: (0, 0, 0))],
        out_specs=pl.BlockSpec((1, blk, GROUP_W), lambda b, i: (b, i, 0)),
        out_shape=jax.ShapeDtypeStruct((bsz, n_lat_tok, GROUP_W), BF16),
        scratch_shapes=[pltpu.VMEM((3 * blk, GROUP_W), BF16), pltpu.VMEM((3 * blk, GROUP_W), BF16)],
        compiler_params=_cparams(("parallel", "parallel")),
        name="nbr",
    )(nq, nk, nk, nk, nv, nv, nv, nk, nv, bias)


def _outproj_kernel(x_ref, mod_ref, yd_ref, ym_ref, hf_ref, hb_ref, lo_ref, yn_ref, gml_ref, g2_ref, w_ref, e64_ref,
                    x1_ref, h2_ref):
    hs = hf_ref[0] + hb_ref[0]
    ms = _group_sumsq(hs, e64_ref) * (1.0 / HEAD_DIM)
    yl = (hs * lax.rsqrt(ms + NORM_EPS) * gml_ref[...] * lo_ref[0].astype(F32)).astype(BF16)
    o = (_dot(yd_ref[0], w_ref[0:256, :]) + _dot(ym_ref[0], w_ref[256:512, :])
         + _dot(yl, w_ref[512:768, :]) + _dot(yn_ref[0], w_ref[768:1024, :]))
    x1 = x_ref[0] + mod_ref[0, 0, 2:3, :] * o
    x1_ref[0] = x1
    ms2 = jnp.mean(x1 * x1, axis=-1, keepdims=True)
    h2 = (x1 * lax.rsqrt(ms2 + NORM_EPS)) * g2_ref[...] * mod_ref[0, 0, 4:5, :] + mod_ref[0, 0, 3:4, :]
    h2_ref[0] = h2.astype(BF16)


def _outproj(x_all, modv, yd, ym, hf, hb, lo, yn, gml, g2, w_out, e64, n_lat, n_tiles):
    bsz, _, d = x_all.shape
    tm = ROW_TILE

    def tok(width):
        return pl.BlockSpec((1, tm, width), lambda i, b: (b, i, 0))

    def full(a):
        return pl.BlockSpec(a.shape, lambda i, b: (0,) * a.ndim)

    return pl.pallas_call(
        _outproj_kernel,
        grid=(n_tiles, bsz),
        in_specs=[tok(d), pl.BlockSpec((1, 1, 8, d), lambda i, b: (b, (i >= n_lat).astype(jnp.int32), 0, 0)),
                  tok(GROUP_W), tok(GROUP_W), tok(GROUP_W), tok(GROUP_W),
                  tok(GROUP_W), tok(GROUP_W), full(gml), full(g2), full(w_out), full(e64)],
        out_specs=[tok(d), tok(d)],
        out_shape=[jax.ShapeDtypeStruct((bsz, n_tiles * tm, d), F32),
                   jax.ShapeDtypeStruct((bsz, n_tiles * tm, d), BF16)],
        compiler_params=_cparams(("parallel", "parallel")),
        name="outproj",
    )(x_all, modv, yd, ym, hf, hb, lo, yn, gml, g2, w_out, e64)


def _ffn_kernel(x1_ref, h_ref, hp_ref, hn_ref, mod_ref, wup_ref, wc_ref, wdn_ref, o_ref, act_scr, *, t_lat, t_all):
    i = pl.program_id(0)
    tm, d = h_ref.shape[1], h_ref.shape[2]
    lhs = jnp.concatenate([hp_ref[0], h_ref[0], hn_ref[0]], axis=0)
    row = lax.broadcasted_iota(jnp.int32, (tm, FFN_CHUNK), 0)
    tok = i * tm + row
    first, last = row == 0, row == tm - 1
    no_prev = (tok == 0) | (tok == t_lat)
    no_next = (tok == t_lat - 1) | (tok == t_all - 1)

    def conv_up(c0):
        uu = _dot(lhs, wup_ref[:, c0:c0 + FFN_CHUNK])
        u = uu[FFN_HALO:FFN_HALO + tm]
        up = jnp.where(first, uu[FFN_HALO - 1:FFN_HALO], pltpu.roll(u, 1, axis=0))
        dn = jnp.where(last, uu[FFN_HALO + tm:FFN_HALO + tm + 1], pltpu.roll(u, tm - 1, axis=0))
        return (wc_ref[0:1, c0:c0 + FFN_CHUNK] * jnp.where(no_prev, 0.0, up) + wc_ref[1:2, c0:c0 + FFN_CHUNK] * u
                + wc_ref[2:3, c0:c0 + FFN_CHUNK] * jnp.where(no_next, 0.0, dn))

    for c in range(FFN_HIDDEN // FFN_CHUNK):
        a = conv_up(c * FFN_CHUNK)
        g = conv_up(FFN_HIDDEN + c * FFN_CHUNK)
        act_scr[:, c * FFN_CHUNK:(c + 1) * FFN_CHUNK] = (g * _sigmoid(g) * a).astype(BF16)
    down = _dot(act_scr[...], wdn_ref[...])
    is_ctx = i * tm + lax.broadcasted_iota(jnp.int32, (tm, d), 0) >= t_lat
    gate = jnp.where(is_ctx, mod_ref[0, 1, 5:6, :], mod_ref[0, 0, 5:6, :])
    o_ref[0] = x1_ref[0] + gate * down


def _ffn(x1, h2, modv, w_up, w_conv, w_down, t_lat, tm):
    bsz, rows, d = x1.shape
    n_tiles = rows // tm
    per = tm // FFN_HALO
    last = rows // FFN_HALO - 1

    def tok():
        return pl.BlockSpec((1, tm, d), lambda i, b: (b, i, 0))

    def resident(a):
        return pl.BlockSpec(a.shape, lambda i, b: (0,) * a.ndim, pipeline_mode=pl.Buffered(1))

    return pl.pallas_call(
        functools.partial(_ffn_kernel, t_lat=t_lat, t_all=rows),
        grid=(n_tiles, bsz),
        in_specs=[tok(), tok(),
                  pl.BlockSpec((1, FFN_HALO, d), lambda i, b: (b, jnp.maximum(i * per - 1, 0), 0)),
                  pl.BlockSpec((1, FFN_HALO, d), lambda i, b: (b, jnp.minimum((i + 1) * per, last), 0)),
                  pl.BlockSpec((1, 2, 8, d), lambda i, b: (b, 0, 0, 0)),
                  resident(w_up), resident(w_conv), resident(w_down)],
        out_specs=tok(),
        out_shape=jax.ShapeDtypeStruct((bsz, rows, d), F32),
        scratch_shapes=[pltpu.VMEM((tm, FFN_HIDDEN), BF16)],
        compiler_params=_cparams(("parallel", "parallel")),
        name="ffn",
    )(x1, h2, h2, h2, modv, w_up, w_conv, w_down)


def _block_ones(width, group):
    idx = np.arange(width) // group
    return jnp.asarray(idx[:, None] == idx[None, :], dtype=BF16)


def _rope_table(t_len, n_ctx):
    nf = MLA_ROPE // 4
    t = jnp.arange(t_len, dtype=jnp.int32)
    row = (t // GRID_W).astype(F32)
    col = (t % GRID_W).astype(F32)
    inv = ROPE_BASE ** (-jnp.arange(nf, dtype=F32) / nf)
    ar, ac = row[:, None] * inv, col[:, None] * inv
    cr, sr, cc, sc = jnp.cos(ar), jnp.sin(ar), jnp.cos(ac), jnp.sin(ac)
    z = jnp.zeros_like(sr)
    cos = jnp.tile(jnp.concatenate([cr, cr, cc, cc], -1), (1, 4))
    sa = jnp.tile(jnp.concatenate([-sr, z, -sc, z], -1), (1, 4))
    sb = jnp.tile(jnp.concatenate([z, sr, z, sc], -1), (1, 4))
    lat = jnp.concatenate([cos, sa, sb], -1)
    ctx = jnp.concatenate([jnp.ones((n_ctx, 128), F32), jnp.zeros((n_ctx, 256), F32)], -1)
    return jnp.concatenate([lat, ctx], 0)


def _pack_w_in(w):
    d = w.shape[0]
    z = lambda n: jnp.zeros((d, n), w.dtype)
    return jnp.concatenate([w[:, 0:1152], z(64), w[:, 1152:1184], z(32), w[:, 1184:2208],
                            w[:, 2208:2224], z(112), w[:, 2224:2992]], axis=1).astype(BF16)


def _pad_heads(v, real, padded):
    lead = v.shape[:-1]
    v = v.reshape(lead + (N_HEADS, real))
    v = jnp.pad(v, [(0, 0)] * len(lead) + [(0, 0), (0, padded - real)])
    return v.reshape(lead + (N_HEADS * padded,))


def _pack_vectors(dgq, dgk, gcq, gckv, mgq, mgk, ngq, ngk, b_i, b_f, w_conv):
    def row(v):
        return jnp.pad(v.astype(F32), (0, 512 - v.shape[0]))[None]

    rows = [
        row(jnp.tile(dgq, 8) * (DIFF_DK ** -0.5 * LOG2E)), row(jnp.tile(dgk, 8)),
        row(gcq), row(gckv),
        row(_pad_heads(jnp.tile(mgq, N_HEADS), MLA_DK, MLA_PAD) * (MLA_DK ** -0.5 * LOG2E)),
        row(_pad_heads(jnp.tile(mgk, N_HEADS), MLA_DK, MLA_PAD)),
        row(jnp.tile(ngq, N_HEADS) * (HEAD_DIM ** -0.5 * LOG2E)), row(jnp.tile(ngk, N_HEADS)),
        row(jnp.concatenate([b_i.reshape(-1), b_f.reshape(-1)])),
        w_conv.astype(F32),
    ]
    rows = jnp.concatenate(rows, axis=0)
    return jnp.pad(rows, ((0, 16 - rows.shape[0]), (0, 0)))


def _nbr_bias(rpb):
    j = np.arange(NA_ROWS)
    delta = np.arange(NA_ROWS)
    roff = j[None, :] - delta[:, None] + (NA_ROWS - 1)
    cidx = np.arange(GRID_W)
    coff = np.clip(cidx[None, :] - cidx[:, None], 1 - NA_COLS, NA_COLS - 1) + (NA_COLS - 1)
    cs = np.clip(cidx - NA_COLS // 2, 0, GRID_W - NA_COLS)
    col_ok = (cidx[None, :] >= cs[:, None]) & (cidx[None, :] < cs[:, None] + NA_COLS)
    sel_r = jnp.asarray(roff[:, :, None] == np.arange(2 * NA_ROWS - 1), F32)
    sel_c = jnp.asarray(coff[:, :, None] == np.arange(2 * NA_COLS - 1), F32)
    bias = jnp.einsum('hrc,djr,qkc->dhqjk', rpb.astype(F32) * LOG2E, sel_r, sel_c, precision=HIGHEST)
    bias = bias + jnp.asarray(np.where(col_ok, 0.0, NEG), F32)[None, None, :, None, :]
    return bias.reshape(NA_ROWS, N_HEADS * GRID_W, NA_ROWS * GRID_W)


def _mlstm_consts():
    expand = np.zeros((2, 128, 2 * GROUP_W), np.float32)
    for dr in range(2):
        for h in range(N_HEADS):
            expand[dr, dr * N_HEADS + h, h * HEAD_DIM:(h + 1) * HEAD_DIM] = 1.0
            expand[dr, 2 * N_HEADS + dr * N_HEADS + h, GROUP_W + h * HEAD_DIM:GROUP_W + (h + 1) * HEAD_DIM] = 1.0
    lower = np.kron(np.eye(ROW_TILE // MLSTM_CHUNK), np.tril(np.ones((MLSTM_CHUNK, MLSTM_CHUNK))))
    return jnp.asarray(expand, BF16), jnp.asarray(np.stack([lower, lower.T]), BF16)


def _row_tile(rows):
    return next(t for t in (1024, 768, 512, 256) if rows % t == 0)


def _kv_tile(total):
    best = 128
    for t in range(128, 1025, 128):
        if total % t == 0:
            best = t
    return best


def kernel(x, c, ctx, c_ctx, w_mod, b_mod, g_norm1, g_norm2, w_in, w_out, diff_g_q, diff_g_k, diff_lam, diff_g_out, mla_g_cq, mla_g_ckv, mla_w_uq, mla_w_ukv, mla_g_q, mla_g_k, mlstm_w_conv, mlstm_b_i, mlstm_b_f, mlstm_g_out, na_g_q, na_g_k, na_rpb, ffn_w_up, ffn_w_conv, ffn_w_down):
    bsz, t_lat, d = x.shape
    n_ctx = ctx.shape[1]
    depth = w_in.shape[0]
    tt = t_lat + n_ctx
    tm = ROW_TILE
    assert d == D_MODEL and t_lat % (NA_ROWS * GRID_W) == 0 and n_ctx % tm == 0 and bsz < 8
    n_lat, n_tiles = t_lat // tm, tt // tm

    c_rows = jnp.zeros((8, d), F32).at[:bsz].set(c).at[bsz].set(c_ctx)
    mod_all = _modulation(c_rows, w_mod, b_mod)
    tab = _rope_table(t_lat, n_ctx)
    consts = (_block_ones(256, DIFF_DK), _block_ones(256, HEAD_DIM), _block_ones(256, MLA_PAD))
    expand, tri = _mlstm_consts()
    tk_lat = _kv_tile(tt)

    x_all = jnp.concatenate([x, ctx], axis=1)
    for l in range(depth):
        need_ctx = l < depth - 1
        lam_init = 0.8 - 0.6 * math.exp(-0.3 * l)
        m = mod_all[l].reshape(8, 6, d)
        lat = jnp.pad(m[:bsz], ((0, 0), (0, 2), (0, 0)))
        cx = jnp.broadcast_to(jnp.pad(m[bsz], ((0, 2), (0, 0)))[None], lat.shape)
        modv = jnp.stack([lat, cx], axis=1)
        modv = modv.at[:, :, 1].add(1.0).at[:, :, 4].add(1.0)

        wukv = mla_w_ukv[l].reshape(MLA_KV_LORA, N_HEADS, MLA_NOPE + HEAD_DIM)
        wuk = jnp.pad(wukv[:, :, :MLA_NOPE], ((0, 0), (0, 0), (0, MLA_PAD - MLA_NOPE))).reshape(MLA_KV_LORA, -1)
        wuv = wukv[:, :, MLA_NOPE:].reshape(MLA_KV_LORA, -1)
        vecs = _pack_vectors(diff_g_q[l], diff_g_k[l], mla_g_cq[l], mla_g_ckv[l], mla_g_q[l], mla_g_k[l],
                             na_g_q[l], na_g_k[l], mlstm_b_i[l], mlstm_b_f[l], mlstm_w_conv[l])
        (dq, dkt, dv, mq, mkt, mv, lq, lk, lv, lo, lg, nq, nk, nv, dkn, mkn) = _inproj(
            x_all, modv, g_norm1[l][None], _pack_w_in(w_in[l]), tab, consts, vecs,
            _pad_heads(mla_w_uq[l], MLA_DK, MLA_PAD).astype(BF16), wuk.astype(BF16), wuv.astype(BF16), n_lat)
        dkmax = jnp.max(dkn[:, :, 0, :], axis=1)[:, None, :]
        mkmax = jnp.max(mkn[:, :, 0, :], axis=1)[:, None, :]

        lam = diff_lam[l].astype(F32)
        gout = jnp.tile(diff_g_out[l].astype(F32), N_HEADS)[None]
        lat_q = dict(q_off=0, n_q=t_lat // ATTN_TQ, tq=ATTN_TQ, k_off=0, n_k=tt // tk_lat, tk=tk_lat)
        ctx_q = dict(q_off=n_lat, n_q=n_ctx // tm, tq=tm, k_off=t_lat // n_ctx, n_k=1, tk=n_ctx)
        diff_cfg = dict(n_maps=2, qs=DIFF_DK, ks=DIFF_KS, aug_lane=DIFF_DK, transposed=True, lam_init=lam_init)
        mla_cfg = dict(n_maps=1, qs=MLA_PAD, ks=MLA_PAD, aug_lane=MLA_DK, transposed=True, lam_init=0.0)
        yd = _attention(dq, dkt, dv, dkmax, lam, gout, **lat_q, **diff_cfg)
        ym = _attention(mq, mkt, mv, mkmax, lam, gout, **lat_q, **mla_cfg)
        yn = _neighbourhood(nq, nk, nv, _nbr_bias(na_rpb[l]), t_lat, n_ctx)
        hf, hb = _mlstm(lq, lk, lv, lg, expand, tri, t_lat)
        if need_ctx:
            na_cfg = dict(n_maps=1, qs=HEAD_DIM, ks=HEAD_DIM, aug_lane=0, transposed=False, lam_init=0.0)
            yd = jnp.concatenate([yd, _attention(dq, dkt, dv, dkmax, lam, gout, **ctx_q, **diff_cfg)], axis=1)
            ym = jnp.concatenate([ym, _attention(mq, mkt, mv, mkmax, lam, gout, **ctx_q, **mla_cfg)], axis=1)
            yn = jnp.concatenate([yn, _attention(nq, nk, nv, dkmax, lam, gout, **ctx_q, **na_cfg)], axis=1)
        rows = n_tiles if need_ctx else n_lat
        x1, h2 = _outproj(x_all, modv, yd, ym, hf, hb, lo, yn, jnp.tile(mlstm_g_out[l].astype(F32), N_HEADS)[None],
                          g_norm2[l][None], w_out[l].astype(BF16), consts[1], n_lat, rows)
        x_all = _ffn(x1, h2, modv, ffn_w_up[l].astype(BF16), ffn_w_conv[l].astype(F32),
                     ffn_w_down[l].astype(BF16), t_lat, _row_tile(rows * tm))
    return x_all[:, :t_lat]
```

```python
import functools
import math

import numpy as np
import jax
import jax.numpy as jnp
from jax import lax
from jax.experimental import pallas as pl
from jax.experimental.pallas import tpu as pltpu

F32 = jnp.float32
BF16 = jnp.bfloat16
HIGHEST = lax.Precision.HIGHEST

D_MODEL = 1024
GRID_W = 64
N_HEADS = 4
HEAD_DIM = 64
GROUP_W = N_HEADS * HEAD_DIM
DIFF_DK = 32
MLA_Q_LORA = 256
MLA_KV_LORA = 128
MLA_NOPE = 64
MLA_ROPE = 32
MLA_DK = MLA_NOPE + MLA_ROPE
MLA_PAD = 128
DIFF_KS = 64
BOUND_LIMIT = 50.0
MLSTM_CHUNK = 64
NA_ROWS = 8
NA_COLS = 16
FFN_HIDDEN = 2816
ROPE_BASE = 10000.0
NORM_EPS = 1e-6
LOG2E = 1.4426950408889634
NEG = -1e30

ROW_TILE = 256
ATTN_TQ = 1024
FFN_CHUNK = 256
FFN_HALO = 16
VMEM_LIMIT = 56 * 1024 * 1024

C_DQ, C_DK, C_DV, C_CQ, C_CKV, C_KR = 0, 256, 512, 768, 1024, 1152
C_LQ, C_LV, C_LO, C_LG, C_NQ, C_NK, C_NV = 1280, 1792, 2048, 2304, 2432, 2688, 2944
W1_COLS = 3200
V_DGQ, V_DGK, V_GCQ, V_GCKV, V_MGQ, V_MGK, V_NGQ, V_NGK, V_GB, V_CW = 0, 1, 2, 3, 4, 5, 6, 7, 8, 9


def _cparams(sem):
    return pltpu.CompilerParams(dimension_semantics=sem, vmem_limit_bytes=VMEM_LIMIT)


def _dot(a, b):
    return jnp.dot(a, b, preferred_element_type=F32)


def _dot_nt(a, b):
    return lax.dot_general(a, b, (((1,), (1,)), ((), ())), preferred_element_type=F32)


def _dot_tn(a, b):
    return lax.dot_general(a, b, (((0,), (0,)), ((), ())), preferred_element_type=F32)


def _dot_hilo(x, w):
    hi = x.astype(BF16)
    lo = (x - hi.astype(F32)).astype(BF16)
    return _dot(hi, w) + _dot(lo, w)


def _sigmoid(x):
    return 1.0 / (1.0 + jnp.exp(-x))


def _group_sumsq(x, e_ref):
    w = x.shape[1]
    sq = (x * x).astype(BF16)
    return jnp.concatenate([_dot(sq[:, c:c + 256], e_ref[...]) for c in range(0, w, 256)], axis=1)


def _mod_kernel(c_ref, w_ref, b_ref, o_ref):
    c = c_ref[...]
    a = c * _sigmoid(c)
    o_ref[0] = _dot(a.astype(BF16), w_ref[0].astype(BF16)) + b_ref[0]


def _modulation(c_rows, w_mod, b_mod):
    n_layers, d, n_out = w_mod.shape
    tn = 1536
    return pl.pallas_call(
        _mod_kernel,
        grid=(n_layers, n_out // tn),
        in_specs=[pl.BlockSpec((8, d), lambda l, j: (0, 0)),
                  pl.BlockSpec((1, d, tn), lambda l, j: (l, 0, j)),
                  pl.BlockSpec((1, 1, tn), lambda l, j: (l, 0, j))],
        out_specs=pl.BlockSpec((1, 8, tn), lambda l, j: (l, 0, j)),
        out_shape=jax.ShapeDtypeStruct((n_layers, 8, n_out), F32),
        compiler_params=_cparams(("parallel", "parallel")),
        name="mod",
    )(c_rows, w_mod, b_mod.reshape(n_layers, 1, n_out))


def _inproj_kernel(x_ref, xp_ref, xn_ref, mod_ref, g1_ref, w_ref, tab_ref, e32_ref, e64_ref, e128_ref,
                   vec_ref, wuq_ref, wuk_ref, wuv_ref,
                   dq_ref, dkt_ref, dv_ref, mq_ref, mkt_ref, mv_ref,
                   lq_ref, lk_ref, lv_ref, lo_ref, lg_ref, nq_ref, nk_ref, nv_ref, dkn_ref, mkn_ref,
                   *, t_lat, t_all):
    i = pl.program_id(0)
    tm = x_ref.shape[1]
    g1 = g1_ref[...]

    def normmod(xv, tok0):
        is_ctx = tok0 + lax.broadcasted_iota(jnp.int32, xv.shape, 0) >= t_lat
        shift = jnp.where(is_ctx, mod_ref[0, 1, 0:1, :], mod_ref[0, 0, 0:1, :])
        scale = jnp.where(is_ctx, mod_ref[0, 1, 1:2, :], mod_ref[0, 0, 1:2, :])
        ms = jnp.mean(xv * xv, axis=-1, keepdims=True)
        return ((xv * lax.rsqrt(ms + NORM_EPS)) * g1 * scale + shift).astype(BF16)

    h = normmod(x_ref[0], i * tm)

    def proj(c0, width):
        return _dot(h, w_ref[:, c0:c0 + width])

    def vec(r, width):
        return vec_ref[r:r + 1, 0:width]

    tab = tab_ref[...]
    cos128, sa128, sb128 = tab[:, 0:128], tab[:, 128:256], tab[:, 256:384]

    def rope(t, cos, sa, sb):
        w = t.shape[1]
        return t * cos + pltpu.roll(t, w - 8, axis=1) * sa + pltpu.roll(t, 8, axis=1) * sb

    def tile_lanes(t, n):
        return jnp.concatenate([t] * n, axis=1)

    cos_d, sa_d, sb_d = tile_lanes(cos128, 2), tile_lanes(sa128, 2), tile_lanes(sb128, 2)
    xq = proj(C_DQ, 256)
    qn = xq * lax.rsqrt(_group_sumsq(xq, e32_ref) * (1.0 / DIFF_DK) + NORM_EPS) * vec(V_DGQ, 256)
    dq_ref[0] = rope(qn, cos_d, sa_d, sb_d).astype(BF16)
    xk = proj(C_DK, 256)
    kn = xk * lax.rsqrt(_group_sumsq(xk, e32_ref) * (1.0 / DIFF_DK) + NORM_EPS) * vec(V_DGK, 256)
    dkn_ref[0, 0] = jnp.broadcast_to(jnp.max(_group_sumsq(kn, e32_ref), axis=0, keepdims=True), (8, 256))
    kt = rope(kn, cos_d, sa_d, sb_d).T
    ones_row = jnp.where(lax.broadcasted_iota(jnp.int32, (DIFF_KS - DIFF_DK, tm), 0) == 0, 1.0, 0.0)
    dkt_ref[0] = jnp.concatenate(
        [piece for mp in range(2 * N_HEADS) for piece in (kt[mp * DIFF_DK:(mp + 1) * DIFF_DK], ones_row)],
        axis=0).astype(BF16)
    dv_ref[0] = proj(C_DV, 256).astype(BF16)

    lane128 = lax.broadcasted_iota(jnp.int32, (tm, 128), 1)
    in_rope = (lane128 >= MLA_NOPE) & (lane128 < MLA_DK)
    cos_m = tile_lanes(jnp.where(in_rope, cos128, 1.0), 4)
    sa_m = tile_lanes(jnp.where(in_rope, sa128, 0.0), 4)
    sb_m = tile_lanes(jnp.where(in_rope, sb128, 0.0), 4)
    cq = proj(C_CQ, 256)
    cqn = cq * lax.rsqrt(jnp.mean(cq * cq, axis=-1, keepdims=True) + NORM_EPS) * vec(V_GCQ, 256)
    q = _dot(cqn.astype(BF16), wuq_ref[...])
    q = q * lax.rsqrt(_group_sumsq(q, e128_ref) * (1.0 / MLA_DK) + NORM_EPS) * vec(V_MGQ, 512)
    mq_ref[0] = rope(q, cos_m, sa_m, sb_m).astype(BF16)
    ckv = proj(C_CKV, 128)
    ckvn = (ckv * lax.rsqrt(jnp.mean(ckv * ckv, axis=-1, keepdims=True) + NORM_EPS) * vec(V_GCKV, 128)).astype(BF16)
    k = _dot(ckvn, wuk_ref[...]) + tile_lanes(proj(C_KR, 128), 4)
    k = k * lax.rsqrt(_group_sumsq(k, e128_ref) * (1.0 / MLA_DK) + NORM_EPS) * vec(V_MGK, 512)
    mkn_ref[0, 0] = jnp.broadcast_to(jnp.max(_group_sumsq(k, e128_ref), axis=0, keepdims=True), (8, 512))
    k = jnp.where(tile_lanes(lane128 == MLA_DK, 4), 1.0, rope(k, cos_m, sa_m, sb_m))
    mkt_ref[0] = k.T.astype(BF16)
    mv_ref[0] = _dot(ckvn, wuv_ref[...]).astype(BF16)

    u = proj(C_LQ, 512)
    w_lqk = w_ref[:, C_LQ:C_LQ + 512]
    u_prev = _dot(normmod(xp_ref[0], i * tm - 8), w_lqk)[7:8, :]
    u_next = _dot(normmod(xn_ref[0], (i + 1) * tm), w_lqk)[0:1, :]
    row = lax.broadcasted_iota(jnp.int32, (tm, 512), 0)
    tok = i * tm + row
    no_prev = (tok == 0) | (tok == t_lat)
    no_next = (tok == t_lat - 1) | (tok == t_all - 1)
    up = jnp.where(no_prev, 0.0, jnp.where(row == 0, u_prev, pltpu.roll(u, 1, axis=0)))
    dn = jnp.where(no_next, 0.0, jnp.where(row == tm - 1, u_next, pltpu.roll(u, tm - 1, axis=0)))
    conv = vec(V_CW, 512) * up + vec(V_CW + 1, 512) * u + vec(V_CW + 2, 512) * dn
    qk = conv * _sigmoid(conv)
    lq_ref[0] = qk[:, 0:256].astype(BF16)
    lk_ref[0] = (qk[:, 256:512] * (HEAD_DIM ** -0.5)).astype(BF16)
    lv_ref[0] = proj(C_LV, 256).astype(BF16)
    lo_ref[0] = _sigmoid(proj(C_LO, 256)).astype(BF16)
    gt = proj(C_LG, 128) + vec(V_GB, 128)
    log_sig = jnp.minimum(gt, 0.0) - jnp.log1p(jnp.exp(-jnp.abs(gt)))
    lg_ref[0] = jnp.where(lane128 < 2 * N_HEADS, gt, log_sig)

    xq = proj(C_NQ, 256)
    nq_ref[0] = (xq * lax.rsqrt(_group_sumsq(xq, e64_ref) * (1.0 / HEAD_DIM) + NORM_EPS) * vec(V_NGQ, 256)).astype(BF16)
    xk = proj(C_NK, 256)
    nk_ref[0] = (xk * lax.rsqrt(_group_sumsq(xk, e64_ref) * (1.0 / HEAD_DIM) + NORM_EPS) * vec(V_NGK, 256)).astype(BF16)
    nv_ref[0] = proj(C_NV, 256).astype(BF16)


def _inproj(x_all, modv, g1, w1, tab, consts, vecs, wuq, wuk, wuv, t_lat):
    bsz, tt, d = x_all.shape
    tm = _row_tile(tt)
    n_tiles = tt // tm
    last8 = tt // 8 - 1

    def full(a):
        return pl.BlockSpec(a.shape, lambda i, b: (0,) * a.ndim)

    def tok(width):
        return pl.BlockSpec((1, tm, width), lambda i, b: (b, i, 0))

    def tok_t(rows):
        return pl.BlockSpec((1, rows, tm), lambda i, b: (b, 0, i))

    def sds(shape, dt=BF16):
        return jax.ShapeDtypeStruct(shape, dt)

    e32, e64, e128 = consts
    outs = [
        (sds((bsz, tt, 256)), tok(256)), (sds((bsz, 8 * DIFF_KS, tt)), tok_t(8 * DIFF_KS)), (sds((bsz, tt, 256)), tok(256)),
        (sds((bsz, tt, 512)), tok(512)), (sds((bsz, 512, tt)), tok_t(512)), (sds((bsz, tt, 256)), tok(256)),
        (sds((bsz, tt, 256)), tok(256)), (sds((bsz, tt, 256)), tok(256)), (sds((bsz, tt, 256)), tok(256)),
        (sds((bsz, tt, 256)), tok(256)), (sds((bsz, tt, 128), F32), tok(128)),
        (sds((bsz, tt, 256)), tok(256)), (sds((bsz, tt, 256)), tok(256)), (sds((bsz, tt, 256)), tok(256)),
        (sds((bsz, n_tiles, 8, 256), F32), pl.BlockSpec((1, 1, 8, 256), lambda i, b: (b, i, 0, 0))),
        (sds((bsz, n_tiles, 8, 512), F32), pl.BlockSpec((1, 1, 8, 512), lambda i, b: (b, i, 0, 0))),
    ]
    return pl.pallas_call(
        functools.partial(_inproj_kernel, t_lat=t_lat, t_all=tt),
        grid=(n_tiles, bsz),
        in_specs=[
            pl.BlockSpec((1, tm, d), lambda i, b: (b, i, 0)),
            pl.BlockSpec((1, 8, d), lambda i, b: (b, jnp.maximum(i * (tm // 8) - 1, 0), 0)),
            pl.BlockSpec((1, 8, d), lambda i, b: (b, jnp.minimum((i + 1) * (tm // 8), last8), 0)),
            pl.BlockSpec((1, 2, 8, d), lambda i, b: (b, 0, 0, 0)),
            full(g1), full(w1),
            pl.BlockSpec((tm, 384), lambda i, b: (i, 0)),
            full(e32), full(e64), full(e128), full(vecs), full(wuq), full(wuk), full(wuv),
        ],
        out_specs=[o[1] for o in outs],
        out_shape=[o[0] for o in outs],
        compiler_params=_cparams(("parallel", "parallel")),
        name="inproj",
    )(x_all, x_all, x_all, modv, g1, w1, tab, e32, e64, e128, vecs, wuq, wuk, wuv)


def _attn_kernel(q_ref, k_ref, v_ref, kmax_ref, lam_ref, gout_ref, o_ref, q_scr, m_scr, l_scr, acc_scr, flag_scr, *,
                 n_maps, qs, ks, aug_lane, transposed, lam_init):
    ki = pl.program_id(2)
    n_hm = N_HEADS * n_maps
    tq = q_ref.shape[1]
    tk = v_ref.shape[1]
    lane_head = lax.broadcasted_iota(jnp.int32, (tq, GROUP_W), 1) // HEAD_DIM

    @pl.when(ki == 0)
    def _init():
        if transposed:
            lane = lax.broadcasted_iota(jnp.int32, (tq, ks), 1)
            worst = jnp.zeros((tq, 1), F32)
            for hm in range(n_hm):
                qh = q_ref[0, :, hm * qs:(hm + 1) * qs].astype(F32)
                bound = jnp.sqrt(jnp.sum(qh * qh, axis=-1, keepdims=True)
                                 * kmax_ref[0, :, hm * qs:hm * qs + 1]) * 1.02 + 0.01
                if ks > qs:
                    qh = jnp.concatenate([qh, jnp.zeros((tq, ks - qs), F32)], axis=1)
                q_scr[hm] = jnp.where(lane == aug_lane, -bound, qh).astype(BF16)
                worst = jnp.maximum(worst, bound)
            flag_scr[0] = (jnp.max(worst) > BOUND_LIMIT).astype(jnp.int32)
        else:
            for hm in range(n_hm):
                q_scr[hm] = jnp.where(lane_head == hm, q_ref[0], jnp.zeros_like(q_ref[0]))
            flag_scr[0] = 1
        m_scr[...] = jnp.full(m_scr.shape, NEG, F32)
        l_scr[...] = jnp.zeros(l_scr.shape, F32)
        acc_scr[...] = jnp.zeros(acc_scr.shape, F32)

    def scores(hm):
        if transposed:
            return _dot(q_scr[hm], k_ref[0, hm * ks:(hm + 1) * ks, :])
        return _dot_nt(q_scr[hm], k_ref[0])

    def lane_partial_sums(p):
        return functools.reduce(lambda a, b: a + b, [p[:, c:c + 128] for c in range(0, tk, 128)])

    def single_pass():
        v = v_ref[0]
        for hm in range(n_hm):
            p = jnp.exp2(scores(hm))
            l_scr[hm] += lane_partial_sums(p)
            acc_scr[hm] += _dot(p.astype(BF16), v)

    def online():
        v = v_ref[0]
        for hm in range(n_hm):
            s = scores(hm)
            m_prev = m_scr[hm]
            m_new = jnp.maximum(m_prev, jnp.max(s, axis=-1, keepdims=True))
            alpha = jnp.exp2(m_prev - m_new)
            p = jnp.exp2(s - m_new)
            l_scr[hm] = alpha * l_scr[hm] + lane_partial_sums(p)
            acc_scr[hm] = alpha * acc_scr[hm] + _dot(p.astype(BF16), v)
            m_scr[hm] = m_new

    if transposed:
        pl.when(flag_scr[0] == 0)(single_pass)
        pl.when(flag_scr[0] != 0)(online)
    else:
        online()

    @pl.when(ki == pl.num_programs(2) - 1)
    def _finish():
        out = jnp.zeros((tq, GROUP_W), F32)
        if n_maps == 2:
            lv = lam_ref[...]
            lam = (jnp.exp(jnp.sum(lv[0:1] * lv[1:2], axis=-1, keepdims=True))
                   - jnp.exp(jnp.sum(lv[2:3] * lv[3:4], axis=-1, keepdims=True)) + lam_init)

        def normalised(hm):
            return acc_scr[hm] / jnp.sum(l_scr[hm], axis=-1, keepdims=True)

        for h in range(N_HEADS):
            if n_maps == 2:
                o = normalised(2 * h) - lam * normalised(2 * h + 1)
                ms = jnp.sum(jnp.where(lane_head == h, o * o, 0.0), axis=-1, keepdims=True) * (1.0 / HEAD_DIM)
                o = o * lax.rsqrt(ms + NORM_EPS) * gout_ref[...] * (1.0 - lam_init)
            else:
                o = normalised(h)
            out = jnp.where(lane_head == h, o, out)
        o_ref[0] = out.astype(o_ref.dtype)


def _attention(q, k, v, kmax, lam, gout, *, q_off, n_q, tq, k_off, n_k, tk, n_maps, qs, ks, aug_lane, transposed,
               lam_init):
    bsz = q.shape[0]
    n_hm = N_HEADS * n_maps
    if transposed:
        k_spec = pl.BlockSpec((1, k.shape[1], tk), lambda b, i, j: (b, 0, j + k_off))
        q_scr = pltpu.VMEM((n_hm, tq, ks), BF16)
    else:
        k_spec = pl.BlockSpec((1, tk, k.shape[2]), lambda b, i, j: (b, j + k_off, 0))
        q_scr = pltpu.VMEM((n_hm, tq, GROUP_W), BF16)
    return pl.pallas_call(
        functools.partial(_attn_kernel, n_maps=n_maps, qs=qs, ks=ks, aug_lane=aug_lane, transposed=transposed,
                          lam_init=lam_init),
        grid=(bsz, n_q, n_k),
        in_specs=[
            pl.BlockSpec((1, tq, q.shape[2]), lambda b, i, j: (b, i + q_off, 0)),
            k_spec,
            pl.BlockSpec((1, tk, GROUP_W), lambda b, i, j: (b, j + k_off, 0)),
            pl.BlockSpec((1, 1, kmax.shape[2]), lambda b, i, j: (b, 0, 0)),
            pl.BlockSpec(lam.shape, lambda b, i, j: (0, 0)),
            pl.BlockSpec(gout.shape, lambda b, i, j: (0, 0)),
        ],
        out_specs=pl.BlockSpec((1, tq, GROUP_W), lambda b, i, j: (b, i, 0)),
        out_shape=jax.ShapeDtypeStruct((bsz, n_q * tq, GROUP_W), BF16),
        scratch_shapes=[q_scr, pltpu.VMEM((n_hm, tq, 1), F32), pltpu.VMEM((n_hm, tq, 128), F32),
                        pltpu.VMEM((n_hm, tq, GROUP_W), F32), pltpu.SMEM((1,), jnp.int32)],
        compiler_params=_cparams(("parallel", "parallel", "arbitrary")),
        name="attn",
    )(q, k, v, kmax, lam, gout)


def _dot_split3(x, w, left):
    hi = x.astype(BF16)
    r1 = x - hi.astype(F32)
    mid = r1.astype(BF16)
    lo = (r1 - mid.astype(F32)).astype(BF16)
    if left:
        return _dot(w, hi) + _dot(w, mid) + _dot(w, lo)
    return _dot(hi, w) + _dot(mid, w) + _dot(lo, w)


def _mlstm_kernel(qf_ref, kf_ref, vf_ref, gf_ref, qb_ref, kb_ref, vb_ref, gb_ref, x_ref, tri_ref,
                  of_ref, ob_ref, ct_scr, n_scr, m_scr, *, n_chunk):
    j = pl.program_id(0)
    lc = MLSTM_CHUNK
    bsz = qf_ref.shape[0]

    @pl.when(j == 0)
    def _init():
        ct_scr[...] = jnp.zeros(ct_scr.shape, F32)
        n_scr[...] = jnp.zeros(n_scr.shape, F32)
        m_scr[...] = jnp.zeros(m_scr.shape, F32)

    row = lax.broadcasted_iota(jnp.int32, (lc, GROUP_W), 0)
    lane = lax.broadcasted_iota(jnp.int32, (lc, GROUP_W), 1)
    pos = lane % lc
    head = lane // HEAD_DIM
    eye_t = pos == row
    blockdiag = (lax.broadcasted_iota(jnp.int32, (GROUP_W, GROUP_W), 0) // HEAD_DIM
                 == lax.broadcasted_iota(jnp.int32, (GROUP_W, GROUP_W), 1) // HEAD_DIM)
    ones_bd = jnp.where(blockdiag, 1.0, 0.0).astype(BF16)

    for dr, (q_ref, k_ref, v_ref, g_ref, o_ref) in enumerate(
            ((qf_ref, kf_ref, vf_ref, gf_ref, of_ref), (qb_ref, kb_ref, vb_ref, gb_ref, ob_ref))):
        causal = (pos <= row) if dr == 0 else (pos >= row)
        expand = x_ref[dr]
        tri = tri_ref[dr]
        for bi in range(bsz):
            ct, n_s, m_s = ct_scr[dr, bi], n_scr[dr, bi], m_scr[dr, bi]
            gates = _dot_split3(g_ref[bi], expand, left=False)
            ig_all, lf_all = gates[:, 0:GROUP_W], gates[:, GROUP_W:2 * GROUP_W]
            b_all = _dot_split3(lf_all, tri, left=True)
            for jj in range(n_chunk):
                c = jj if dr == 0 else n_chunk - 1 - jj
                sl = slice(c * lc, (c + 1) * lc)
                q, k, v = q_ref[bi, sl, :], k_ref[bi, sl, :], v_ref[bi, sl, :]
                ig, lf, b = ig_all[sl], lf_all[sl], b_all[sl]
                g = jnp.sum(lf, axis=0, keepdims=True)
                a = g - b + ig
                m_loc = jnp.max(a, axis=0, keepdims=True)

                r_row = jnp.sum(jnp.where(eye_t, b - ig, 0.0), axis=0, keepdims=True)
                dlog = jnp.where(causal, b - r_row, NEG)
                inter = b + m_s
                m_t = inter
                for h in range(N_HEADS):
                    mh = jnp.max(jnp.where(head == h, dlog, NEG), axis=1, keepdims=True)
                    m_t = jnp.where(head == h, jnp.maximum(m_t, mh), m_t)
                k_bd = jnp.where(blockdiag, jnp.concatenate([k] * N_HEADS, axis=0), jnp.zeros((), BF16))
                v_bd = jnp.where(blockdiag, jnp.concatenate([v] * N_HEADS, axis=0), jnp.zeros((), BF16))
                s = _dot_nt(q, k_bd) * jnp.exp(dlog - m_t)
                e = jnp.exp(inter - m_t)
                s_hi = s.astype(BF16)
                s_lo = (s - s_hi.astype(F32)).astype(BF16)
                num = _dot(s_hi, v_bd) + e * _dot(q, ct.astype(BF16))
                den = _dot(s_hi, ones_bd) + _dot(s_lo, ones_bd) + e * _dot_hilo(q.astype(F32) * n_s, ones_bd)
                o_ref[bi, sl, :] = num / jnp.maximum(jnp.abs(den), jnp.exp(-m_t))

                m_new = jnp.maximum(g + m_s, m_loc)
                a_old = jnp.exp(g + m_s - m_new)
                kw = k.astype(F32) * jnp.exp(a - m_new)
                ct = a_old * ct + jnp.where(blockdiag, _dot_tn(kw.astype(BF16), v), 0.0)
                n_s = a_old * n_s + jnp.sum(kw, axis=0, keepdims=True)
                m_s = m_new
            ct_scr[dr, bi] = ct
            n_scr[dr, bi] = n_s
            m_scr[dr, bi] = m_s


def _mlstm(lq, lk, lv, lg, expand, tri, n_lat_tok):
    bsz, tt, _ = lq.shape
    tb = ROW_TILE
    nb = tt // tb
    nbl = n_lat_tok // tb
    nbc = nb - nbl

    def fwd(j):
        return jnp.where(j < nbc, j + nbl, j - nbc)

    def bwd(j):
        return jnp.where(j < nbc, nbl + nbc - 1 - j, nb - 1 - j)

    def tok(width, order):
        return pl.BlockSpec((bsz, tb, width), lambda j: (0, order(j), 0))

    def full(a):
        return pl.BlockSpec(a.shape, lambda j: (0,) * a.ndim)

    out = jax.ShapeDtypeStruct((bsz, tt, GROUP_W), F32)
    return pl.pallas_call(
        functools.partial(_mlstm_kernel, n_chunk=tb // MLSTM_CHUNK),
        grid=(nb,),
        in_specs=[tok(GROUP_W, fwd), tok(GROUP_W, fwd), tok(GROUP_W, fwd), tok(128, fwd),
                  tok(GROUP_W, bwd), tok(GROUP_W, bwd), tok(GROUP_W, bwd), tok(128, bwd),
                  full(expand), full(tri)],
        out_specs=[tok(GROUP_W, fwd), tok(GROUP_W, bwd)],
        out_shape=[out, out],
        scratch_shapes=[pltpu.VMEM((2, bsz, GROUP_W, GROUP_W), F32), pltpu.VMEM((2, bsz, 1, GROUP_W), F32),
                        pltpu.VMEM((2, bsz, 1, GROUP_W), F32)],
        compiler_params=_cparams(("arbitrary",)),
        name="mlstm",
    )(lq, lk, lv, lg, lq, lk, lv, lg, expand, tri)


def _nbr_kernel(q_ref, kp_ref, kc_ref, kn_ref, vp_ref, vc_ref, vn_ref, kx_ref, vx_ref, bias_ref, o_ref,
                kw_scr, vw_scr, *, n_rows):
    i = pl.program_id(1)
    blk = NA_ROWS * GRID_W
    kw_scr[0:blk] = kp_ref[0]
    kw_scr[blk:2 * blk] = kc_ref[0]
    kw_scr[2 * blk:3 * blk] = kn_ref[0]
    vw_scr[0:blk] = vp_ref[0]
    vw_scr[blk:2 * blk] = vc_ref[0]
    vw_scr[2 * blk:3 * blk] = vn_ref[0]
    kx = kx_ref[0]
    vx = vx_ref[0]
    lane_head = lax.broadcasted_iota(jnp.int32, (GRID_W, GROUP_W), 1) // HEAD_DIM
    blockdiag = (lax.broadcasted_iota(jnp.int32, (GROUP_W, GROUP_W), 0) // GRID_W
                 == lax.broadcasted_iota(jnp.int32, (GROUP_W, GROUP_W), 1) // HEAD_DIM)

    for rr in range(NA_ROWS):
        r = i * NA_ROWS + rr
        rs = jnp.clip(r - NA_ROWS // 2, 0, n_rows - NA_ROWS)
        off = pl.multiple_of((rs - (i - 1) * NA_ROWS) * GRID_W, GRID_W)
        delta = r - rs
        q = q_ref[0, rr * GRID_W:(rr + 1) * GRID_W, :]
        kw = kw_scr[pl.ds(off, blk), :]
        vw = vw_scr[pl.ds(off, blk), :]
        q_bd = jnp.where(blockdiag, jnp.concatenate([q] * N_HEADS, axis=0), jnp.zeros((), BF16))
        sw = _dot_nt(q_bd, kw) + bias_ref[delta]
        sx = _dot_nt(q_bd, kx)
        m = jnp.maximum(jnp.max(sw, axis=-1, keepdims=True), jnp.max(sx, axis=-1, keepdims=True))
        pw = jnp.exp2(sw - m)
        px = jnp.exp2(sx - m)
        l = jnp.sum(pw, axis=-1, keepdims=True) + jnp.sum(px, axis=-1, keepdims=True)
        o = (_dot(pw.astype(BF16), vw) + _dot(px.astype(BF16), vx)) / l
        out = jnp.zeros((GRID_W, GROUP_W), F32)
        for h in range(N_HEADS):
            out = jnp.where(lane_head == h, o[h * GRID_W:(h + 1) * GRID_W], out)
        o_ref[0, rr * GRID_W:(rr + 1) * GRID_W, :] = out.astype(o_ref.dtype)


def _neighbourhood(nq, nk, nv, bias, n_lat_tok, n_ctx_tok):
    bsz = nq.shape[0]
    n_rows = n_lat_tok // GRID_W
    blk = NA_ROWS * GRID_W
    nblk = n_lat_tok // blk
    ctx_blk = n_lat_tok // n_ctx_tok

    def win(shift):
        return pl.BlockSpec((1, blk, GROUP_W), lambda b, i: (b, jnp.clip(i + shift, 0, nblk - 1), 0))

    ctx_spec = pl.BlockSpec((1, n_ctx_tok, GROUP_W), lambda b, i: (b, ctx_blk, 0))
    return pl.pallas_call(
        functools.partial(_nbr_kernel, n_rows=n_rows),
        grid=(bsz, nblk),
        in_specs=[win(0), win(-1), win(0), win(1), win(-1), win(0), win(1), ctx_spec, ctx_spec,
                  pl.BlockSpec(bias.shape, lambda b, i: (0, 0, 0))],
        out_specs=pl.BlockSpec((1, blk, GROUP_W), lambda b, i: (b, i, 0)),
        out_shape=jax.ShapeDtypeStruct((bsz, n_lat_tok, GROUP_W), BF16),
        scratch_shapes=[pltpu.VMEM((3 * blk, GROUP_W), BF16), pltpu.VMEM((3 * blk, GROUP_W), BF16)],
        compiler_params=_cparams(("parallel", "parallel")),
        name="nbr",
    )(nq, nk, nk, nk, nv, nv, nv, nk, nv, bias)


def _outproj_kernel(x_ref, mod_ref, yd_ref, ym_ref, hf_ref, hb_ref, lo_ref, yn_ref, gml_ref, g2_ref, w_ref, e64_ref,
                    x1_ref, h2_ref, *, t_lat):
    tm, d = x_ref.shape[1], x_ref.shape[2]
    is_ctx = pl.program_id(0) * tm + lax.broadcasted_iota(jnp.int32, (tm, d), 0) >= t_lat

    def mod(r):
        return jnp.where(is_ctx, mod_ref[0, 1, r:r + 1, :], mod_ref[0, 0, r:r + 1, :])

    hs = hf_ref[0] + hb_ref[0]
    ms = _group_sumsq(hs, e64_ref) * (1.0 / HEAD_DIM)
    yl = (hs * lax.rsqrt(ms + NORM_EPS) * gml_ref[...] * lo_ref[0].astype(F32)).astype(BF16)
    o = (_dot(yd_ref[0], w_ref[0:256, :]) + _dot(ym_ref[0], w_ref[256:512, :])
         + _dot(yl, w_ref[512:768, :]) + _dot(yn_ref[0], w_ref[768:1024, :]))
    x1 = x_ref[0] + mod(2) * o
    x1_ref[0] = x1
    ms2 = jnp.mean(x1 * x1, axis=-1, keepdims=True)
    h2 = (x1 * lax.rsqrt(ms2 + NORM_EPS)) * g2_ref[...] * mod(4) + mod(3)
    h2_ref[0] = h2.astype(BF16)


def _outproj(x_all, modv, yd, ym, hf, hb, lo, yn, gml, g2, w_out, e64, t_lat, rows):
    bsz, _, d = x_all.shape
    tm = _row_tile(rows)
    n_tiles = rows // tm

    def tok(width):
        return pl.BlockSpec((1, tm, width), lambda i, b: (b, i, 0))

    def full(a):
        return pl.BlockSpec(a.shape, lambda i, b: (0,) * a.ndim)

    return pl.pallas_call(
        functools.partial(_outproj_kernel, t_lat=t_lat),
        grid=(n_tiles, bsz),
        in_specs=[tok(d), pl.BlockSpec((1, 2, 8, d), lambda i, b: (b, 0, 0, 0)),
                  tok(GROUP_W), tok(GROUP_W), tok(GROUP_W), tok(GROUP_W),
                  tok(GROUP_W), tok(GROUP_W), full(gml), full(g2), full(w_out), full(e64)],
        out_specs=[tok(d), tok(d)],
        out_shape=[jax.ShapeDtypeStruct((bsz, n_tiles * tm, d), F32),
                   jax.ShapeDtypeStruct((bsz, n_tiles * tm, d), BF16)],
        compiler_params=_cparams(("parallel", "parallel")),
        name="outproj",
    )(x_all, modv, yd, ym, hf, hb, lo, yn, gml, g2, w_out, e64)


def _ffn_kernel(x1_ref, h_ref, hp_ref, hn_ref, mod_ref, wup_ref, wc_ref, wdn_ref, o_ref, act_scr, *, t_lat, t_all):
    i = pl.program_id(0)
    tm, d = h_ref.shape[1], h_ref.shape[2]
    lhs = jnp.concatenate([hp_ref[0], h_ref[0], hn_ref[0]], axis=0)
    row = lax.broadcasted_iota(jnp.int32, (tm, FFN_CHUNK), 0)
    tok = i * tm + row
    first, last = row == 0, row == tm - 1
    no_prev = (tok == 0) | (tok == t_lat)
    no_next = (tok == t_lat - 1) | (tok == t_all - 1)

    def conv_up(c0):
        uu = _dot(lhs, wup_ref[:, c0:c0 + FFN_CHUNK])
        u = uu[FFN_HALO:FFN_HALO + tm]
        up = jnp.where(first, uu[FFN_HALO - 1:FFN_HALO], pltpu.roll(u, 1, axis=0))
        dn = jnp.where(last, uu[FFN_HALO + tm:FFN_HALO + tm + 1], pltpu.roll(u, tm - 1, axis=0))
        return (wc_ref[0:1, c0:c0 + FFN_CHUNK] * jnp.where(no_prev, 0.0, up) + wc_ref[1:2, c0:c0 + FFN_CHUNK] * u
                + wc_ref[2:3, c0:c0 + FFN_CHUNK] * jnp.where(no_next, 0.0, dn))

    for c in range(FFN_HIDDEN // FFN_CHUNK):
        a = conv_up(c * FFN_CHUNK)
        g = conv_up(FFN_HIDDEN + c * FFN_CHUNK)
        act_scr[:, c * FFN_CHUNK:(c + 1) * FFN_CHUNK] = (g * _sigmoid(g) * a).astype(BF16)
    down = _dot(act_scr[...], wdn_ref[...])
    is_ctx = i * tm + lax.broadcasted_iota(jnp.int32, (tm, d), 0) >= t_lat
    gate = jnp.where(is_ctx, mod_ref[0, 1, 5:6, :], mod_ref[0, 0, 5:6, :])
    o_ref[0] = x1_ref[0] + gate * down


def _ffn(x1, h2, modv, w_up, w_conv, w_down, t_lat, tm):
    bsz, rows, d = x1.shape
    n_tiles = rows // tm
    per = tm // FFN_HALO
    last = rows // FFN_HALO - 1

    def tok():
        return pl.BlockSpec((1, tm, d), lambda i, b: (b, i, 0))

    def resident(a):
        return pl.BlockSpec(a.shape, lambda i, b: (0,) * a.ndim, pipeline_mode=pl.Buffered(1))

    return pl.pallas_call(
        functools.partial(_ffn_kernel, t_lat=t_lat, t_all=rows),
        grid=(n_tiles, bsz),
        in_specs=[tok(), tok(),
                  pl.BlockSpec((1, FFN_HALO, d), lambda i, b: (b, jnp.maximum(i * per - 1, 0), 0)),
                  pl.BlockSpec((1, FFN_HALO, d), lambda i, b: (b, jnp.minimum((i + 1) * per, last), 0)),
                  pl.BlockSpec((1, 2, 8, d), lambda i, b: (b, 0, 0, 0)),
                  resident(w_up), resident(w_conv), resident(w_down)],
        out_specs=tok(),
        out_shape=jax.ShapeDtypeStruct((bsz, rows, d), F32),
        scratch_shapes=[pltpu.VMEM((tm, FFN_HIDDEN), BF16)],
        compiler_params=_cparams(("parallel", "parallel")),
        name="ffn",
    )(x1, h2, h2, h2, modv, w_up, w_conv, w_down)


def _block_ones(width, group):
    idx = np.arange(width) // group
    return jnp.asarray(idx[:, None] == idx[None, :], dtype=BF16)


def _rope_table(t_len, n_ctx):
    nf = MLA_ROPE // 4
    t = jnp.arange(t_len, dtype=jnp.int32)
    row = (t // GRID_W).astype(F32)
    col = (t % GRID_W).astype(F32)
    inv = ROPE_BASE ** (-jnp.arange(nf, dtype=F32) / nf)
    ar, ac = row[:, None] * inv, col[:, None] * inv
    cr, sr, cc, sc = jnp.cos(ar), jnp.sin(ar), jnp.cos(ac), jnp.sin(ac)
    z = jnp.zeros_like(sr)
    cos = jnp.tile(jnp.concatenate([cr, cr, cc, cc], -1), (1, 4))
    sa = jnp.tile(jnp.concatenate([-sr, z, -sc, z], -1), (1, 4))
    sb = jnp.tile(jnp.concatenate([z, sr, z, sc], -1), (1, 4))
    lat = jnp.concatenate([cos, sa, sb], -1)
    ctx = jnp.concatenate([jnp.ones((n_ctx, 128), F32), jnp.zeros((n_ctx, 256), F32)], -1)
    return jnp.concatenate([lat, ctx], 0)


def _pack_w_in(w):
    d = w.shape[0]
    z = lambda n: jnp.zeros((d, n), w.dtype)
    return jnp.concatenate([w[:, 0:1152], z(64), w[:, 1152:1184], z(32), w[:, 1184:2208],
                            w[:, 2208:2224], z(112), w[:, 2224:2992]], axis=1).astype(BF16)


def _pad_heads(v, real, padded):
    lead = v.shape[:-1]
    v = v.reshape(lead + (N_HEADS, real))
    v = jnp.pad(v, [(0, 0)] * len(lead) + [(0, 0), (0, padded - real)])
    return v.reshape(lead + (N_HEADS * padded,))


def _pack_vectors(dgq, dgk, gcq, gckv, mgq, mgk, ngq, ngk, b_i, b_f, w_conv):
    def row(v):
        return jnp.pad(v.astype(F32), (0, 512 - v.shape[0]))[None]

    rows = [
        row(jnp.tile(dgq, 8) * (DIFF_DK ** -0.5 * LOG2E)), row(jnp.tile(dgk, 8)),
        row(gcq), row(gckv),
        row(_pad_heads(jnp.tile(mgq, N_HEADS), MLA_DK, MLA_PAD) * (MLA_DK ** -0.5 * LOG2E)),
        row(_pad_heads(jnp.tile(mgk, N_HEADS), MLA_DK, MLA_PAD)),
        row(jnp.tile(ngq, N_HEADS) * (HEAD_DIM ** -0.5 * LOG2E)), row(jnp.tile(ngk, N_HEADS)),
        row(jnp.concatenate([b_i.reshape(-1), b_f.reshape(-1)])),
        w_conv.astype(F32),
    ]
    rows = jnp.concatenate(rows, axis=0)
    return jnp.pad(rows, ((0, 16 - rows.shape[0]), (0, 0)))


def _nbr_bias(rpb):
    j = np.arange(NA_ROWS)
    delta = np.arange(NA_ROWS)
    roff = j[None, :] - delta[:, None] + (NA_ROWS - 1)
    cidx = np.arange(GRID_W)
    coff = np.clip(cidx[None, :] - cidx[:, None], 1 - NA_COLS, NA_COLS - 1) + (NA_COLS - 1)
    cs = np.clip(cidx - NA_COLS // 2, 0, GRID_W - NA_COLS)
    col_ok = (cidx[None, :] >= cs[:, None]) & (cidx[None, :] < cs[:, None] + NA_COLS)
    sel_r = jnp.asarray(roff[:, :, None] == np.arange(2 * NA_ROWS - 1), F32)
    sel_c = jnp.asarray(coff[:, :, None] == np.arange(2 * NA_COLS - 1), F32)
    bias = jnp.einsum('hrc,djr,qkc->dhqjk', rpb.astype(F32) * LOG2E, sel_r, sel_c, precision=HIGHEST)
    bias = bias + jnp.asarray(np.where(col_ok, 0.0, NEG), F32)[None, None, :, None, :]
    return bias.reshape(NA_ROWS, N_HEADS * GRID_W, NA_ROWS * GRID_W)


def _mlstm_consts():
    expand = np.zeros((2, 128, 2 * GROUP_W), np.float32)
    for dr in range(2):
        for h in range(N_HEADS):
            expand[dr, dr * N_HEADS + h, h * HEAD_DIM:(h + 1) * HEAD_DIM] = 1.0
            expand[dr, 2 * N_HEADS + dr * N_HEADS + h, GROUP_W + h * HEAD_DIM:GROUP_W + (h + 1) * HEAD_DIM] = 1.0
    lower = np.kron(np.eye(ROW_TILE // MLSTM_CHUNK), np.tril(np.ones((MLSTM_CHUNK, MLSTM_CHUNK))))
    return jnp.asarray(expand, BF16), jnp.asarray(np.stack([lower, lower.T]), BF16)


def _row_tile(rows):
    return next(t for t in (1024, 768, 512, 256) if rows % t == 0)


def _kv_tile(total):
    best = 128
    for t in range(128, 1025, 128):
        if total % t == 0:
            best = t
    return best


def kernel(x, c, ctx, c_ctx, w_mod, b_mod, g_norm1, g_norm2, w_in, w_out, diff_g_q, diff_g_k, diff_lam, diff_g_out, mla_g_cq, mla_g_ckv, mla_w_uq, mla_w_ukv, mla_g_q, mla_g_k, mlstm_w_conv, mlstm_b_i, mlstm_b_f, mlstm_g_out, na_g_q, na_g_k, na_rpb, ffn_w_up, ffn_w_conv, ffn_w_down):
    bsz, t_lat, d = x.shape
    n_ctx = ctx.shape[1]
    depth = w_in.shape[0]
    tt = t_lat + n_ctx
    tm = ROW_TILE
    assert d == D_MODEL and t_lat % (NA_ROWS * GRID_W) == 0 and n_ctx % tm == 0 and bsz < 8
    n_lat, n_tiles = t_lat // tm, tt // tm

    c_rows = jnp.zeros((8, d), F32).at[:bsz].set(c).at[bsz].set(c_ctx)
    mod_all = _modulation(c_rows, w_mod, b_mod)
    tab = _rope_table(t_lat, n_ctx)
    consts = (_block_ones(256, DIFF_DK), _block_ones(256, HEAD_DIM), _block_ones(256, MLA_PAD))
    expand, tri = _mlstm_consts()
    tk_lat = _kv_tile(tt)

    x_all = jnp.concatenate([x, ctx], axis=1)
    for l in range(depth):
        need_ctx = l < depth - 1
        lam_init = 0.8 - 0.6 * math.exp(-0.3 * l)
        m = mod_all[l].reshape(8, 6, d)
        lat = jnp.pad(m[:bsz], ((0, 0), (0, 2), (0, 0)))
        cx = jnp.broadcast_to(jnp.pad(m[bsz], ((0, 2), (0, 0)))[None], lat.shape)
        modv = jnp.stack([lat, cx], axis=1)
        modv = modv.at[:, :, 1].add(1.0).at[:, :, 4].add(1.0)

        wukv = mla_w_ukv[l].reshape(MLA_KV_LORA, N_HEADS, MLA_NOPE + HEAD_DIM)
        wuk = jnp.pad(wukv[:, :, :MLA_NOPE], ((0, 0), (0, 0), (0, MLA_PAD - MLA_NOPE))).reshape(MLA_KV_LORA, -1)
        wuv = wukv[:, :, MLA_NOPE:].reshape(MLA_KV_LORA, -1)
        vecs = _pack_vectors(diff_g_q[l], diff_g_k[l], mla_g_cq[l], mla_g_ckv[l], mla_g_q[l], mla_g_k[l],
                             na_g_q[l], na_g_k[l], mlstm_b_i[l], mlstm_b_f[l], mlstm_w_conv[l])
        (dq, dkt, dv, mq, mkt, mv, lq, lk, lv, lo, lg, nq, nk, nv, dkn, mkn) = _inproj(
            x_all, modv, g_norm1[l][None], _pack_w_in(w_in[l]), tab, consts, vecs,
            _pad_heads(mla_w_uq[l], MLA_DK, MLA_PAD).astype(BF16), wuk.astype(BF16), wuv.astype(BF16), t_lat)
        dkmax = jnp.max(dkn[:, :, 0, :], axis=1)[:, None, :]
        mkmax = jnp.max(mkn[:, :, 0, :], axis=1)[:, None, :]

        lam = diff_lam[l].astype(F32)
        gout = jnp.tile(diff_g_out[l].astype(F32), N_HEADS)[None]
        lat_q = dict(q_off=0, n_q=t_lat // ATTN_TQ, tq=ATTN_TQ, k_off=0, n_k=tt // tk_lat, tk=tk_lat)
        ctx_q = dict(q_off=n_lat, n_q=n_ctx // tm, tq=tm, k_off=t_lat // n_ctx, n_k=1, tk=n_ctx)
        diff_cfg = dict(n_maps=2, qs=DIFF_DK, ks=DIFF_KS, aug_lane=DIFF_DK, transposed=True, lam_init=lam_init)
        mla_cfg = dict(n_maps=1, qs=MLA_PAD, ks=MLA_PAD, aug_lane=MLA_DK, transposed=True, lam_init=0.0)
        yd = _attention(dq, dkt, dv, dkmax, lam, gout, **lat_q, **diff_cfg)
        ym = _attention(mq, mkt, mv, mkmax, lam, gout, **lat_q, **mla_cfg)
        yn = _neighbourhood(nq, nk, nv, _nbr_bias(na_rpb[l]), t_lat, n_ctx)
        hf, hb = _mlstm(lq, lk, lv, lg, expand, tri, t_lat)
        if need_ctx:
            na_cfg = dict(n_maps=1, qs=HEAD_DIM, ks=HEAD_DIM, aug_lane=0, transposed=False, lam_init=0.0)
            yd = jnp.concatenate([yd, _attention(dq, dkt, dv, dkmax, lam, gout, **ctx_q, **diff_cfg)], axis=1)
            ym = jnp.concatenate([ym, _attention(mq, mkt, mv, mkmax, lam, gout, **ctx_q, **mla_cfg)], axis=1)
            yn = jnp.concatenate([yn, _attention(nq, nk, nv, dkmax, lam, gout, **ctx_q, **na_cfg)], axis=1)
        rows = tt if need_ctx else t_lat
        x1, h2 = _outproj(x_all, modv, yd, ym, hf, hb, lo, yn, jnp.tile(mlstm_g_out[l].astype(F32), N_HEADS)[None],
                          g_norm2[l][None], w_out[l].astype(BF16), consts[1], t_lat, rows)
        x_all = _ffn(x1, h2, modv, ffn_w_up[l].astype(BF16), ffn_w_conv[l].astype(F32),
                     ffn_w_down[l].astype(BF16), t_lat, _row_tile(rows))
    return x_all[:, :t_lat]
```

```python
import functools
import math

import numpy as np
import jax
import jax.numpy as jnp
from jax import lax
from jax.experimental import pallas as pl
from jax.experimental.pallas import tpu as pltpu

F32 = jnp.float32
BF16 = jnp.bfloat16
HIGHEST = lax.Precision.HIGHEST

D_MODEL = 1024
GRID_W = 64
N_HEADS = 4
HEAD_DIM = 64
GROUP_W = N_HEADS * HEAD_DIM
DIFF_DK = 32
MLA_Q_LORA = 256
MLA_KV_LORA = 128
MLA_NOPE = 64
MLA_ROPE = 32
MLA_DK = MLA_NOPE + MLA_ROPE
MLA_PAD = 128
DIFF_KS = 64
BOUND_LIMIT = 50.0
MLSTM_CHUNK = 64
NA_ROWS = 8
NA_COLS = 16
FFN_HIDDEN = 2816
ROPE_BASE = 10000.0
NORM_EPS = 1e-6
LOG2E = 1.4426950408889634
NEG = -1e30

ROW_TILE = 256
ATTN_TQ = 1024
FFN_CHUNK = 256
FFN_HALO = 16
VMEM_LIMIT = 56 * 1024 * 1024

C_DQ, C_DK, C_DV, C_CQ, C_CKV, C_KR = 0, 256, 512, 768, 1024, 1152
C_LQ, C_LV, C_LO, C_LG, C_NQ, C_NK, C_NV = 1280, 1792, 2048, 2304, 2432, 2688, 2944
W1_COLS = 3200
V_DGQ, V_DGK, V_GCQ, V_GCKV, V_MGQ, V_MGK, V_NGQ, V_NGK, V_GB, V_CW = 0, 1, 2, 3, 4, 5, 6, 7, 8, 9


def _cparams(sem):
    return pltpu.CompilerParams(dimension_semantics=sem, vmem_limit_bytes=VMEM_LIMIT)


def _dot(a, b):
    return jnp.dot(a, b, preferred_element_type=F32)


def _dot_nt(a, b):
    return lax.dot_general(a, b, (((1,), (1,)), ((), ())), preferred_element_type=F32)


def _dot_tn(a, b):
    return lax.dot_general(a, b, (((0,), (0,)), ((), ())), preferred_element_type=F32)


def _dot_hilo(x, w):
    hi = x.astype(BF16)
    lo = (x - hi.astype(F32)).astype(BF16)
    return _dot(hi, w) + _dot(lo, w)


def _sigmoid(x):
    return 1.0 / (1.0 + jnp.exp(-x))


def _group_sumsq(x, e_ref):
    w = x.shape[1]
    sq = (x * x).astype(BF16)
    return jnp.concatenate([_dot(sq[:, c:c + 256], e_ref[...]) for c in range(0, w, 256)], axis=1)


def _mod_kernel(c_ref, w_ref, b_ref, o_ref):
    c = c_ref[...]
    a = c * _sigmoid(c)
    o_ref[0] = _dot(a.astype(BF16), w_ref[0].astype(BF16)) + b_ref[0]


def _modulation(c_rows, w_mod, b_mod):
    n_layers, d, n_out = w_mod.shape
    tn = 1536
    return pl.pallas_call(
        _mod_kernel,
        grid=(n_layers, n_out // tn),
        in_specs=[pl.BlockSpec((8, d), lambda l, j: (0, 0)),
                  pl.BlockSpec((1, d, tn), lambda l, j: (l, 0, j)),
                  pl.BlockSpec((1, 1, tn), lambda l, j: (l, 0, j))],
        out_specs=pl.BlockSpec((1, 8, tn), lambda l, j: (l, 0, j)),
        out_shape=jax.ShapeDtypeStruct((n_layers, 8, n_out), F32),
        compiler_params=_cparams(("parallel", "parallel")),
        name="mod",
    )(c_rows, w_mod, b_mod.reshape(n_layers, 1, n_out))


def _inproj_kernel(x_ref, xp_ref, xn_ref, mod_ref, g1_ref, w_ref, tab_ref, e32_ref, e64_ref, e128_ref,
                   vec_ref, wuq_ref, wuk_ref, wuv_ref,
                   dq_ref, dkt_ref, dv_ref, mq_ref, mkt_ref, mv_ref,
                   lq_ref, lk_ref, lv_ref, lo_ref, lg_ref, nq_ref, nk_ref, nv_ref, *, t_lat, t_all):
    i = pl.program_id(0)
    tm = x_ref.shape[1]
    g1 = g1_ref[...]

    def normmod(xv, tok0):
        is_ctx = tok0 + lax.broadcasted_iota(jnp.int32, xv.shape, 0) >= t_lat
        shift = jnp.where(is_ctx, mod_ref[0, 1, 0:1, :], mod_ref[0, 0, 0:1, :])
        scale = jnp.where(is_ctx, mod_ref[0, 1, 1:2, :], mod_ref[0, 0, 1:2, :])
        ms = jnp.mean(xv * xv, axis=-1, keepdims=True)
        return ((xv * lax.rsqrt(ms + NORM_EPS)) * g1 * scale + shift).astype(BF16)

    h = normmod(x_ref[0], i * tm)

    def proj(c0, width):
        return _dot(h, w_ref[:, c0:c0 + width])

    def vec(r, width):
        return vec_ref[r:r + 1, 0:width]

    tab = tab_ref[...]
    cos128, sa128, sb128 = tab[:, 0:128], tab[:, 128:256], tab[:, 256:384]

    def rope(t, cos, sa, sb):
        w = t.shape[1]
        return t * cos + pltpu.roll(t, w - 8, axis=1) * sa + pltpu.roll(t, 8, axis=1) * sb

    def tile_lanes(t, n):
        return jnp.concatenate([t] * n, axis=1)

    cos_d, sa_d, sb_d = tile_lanes(cos128, 2), tile_lanes(sa128, 2), tile_lanes(sb128, 2)
    xq = proj(C_DQ, 256)
    qn = xq * lax.rsqrt(_group_sumsq(xq, e32_ref) * (1.0 / DIFF_DK) + NORM_EPS) * vec(V_DGQ, 256)
    dq_ref[0] = rope(qn, cos_d, sa_d, sb_d).astype(BF16)
    xk = proj(C_DK, 256)
    kn = xk * lax.rsqrt(_group_sumsq(xk, e32_ref) * (1.0 / DIFF_DK) + NORM_EPS) * vec(V_DGK, 256)
    kt = rope(kn, cos_d, sa_d, sb_d).T
    ones_row = jnp.where(lax.broadcasted_iota(jnp.int32, (DIFF_KS - DIFF_DK, tm), 0) == 0, 1.0, 0.0)
    dkt_ref[0] = jnp.concatenate(
        [piece for mp in range(2 * N_HEADS) for piece in (kt[mp * DIFF_DK:(mp + 1) * DIFF_DK], ones_row)],
        axis=0).astype(BF16)
    dv_ref[0] = proj(C_DV, 256).astype(BF16)

    lane128 = lax.broadcasted_iota(jnp.int32, (tm, 128), 1)
    in_rope = (lane128 >= MLA_NOPE) & (lane128 < MLA_DK)
    cos_m = tile_lanes(jnp.where(in_rope, cos128, 1.0), 4)
    sa_m = tile_lanes(jnp.where(in_rope, sa128, 0.0), 4)
    sb_m = tile_lanes(jnp.where(in_rope, sb128, 0.0), 4)
    cq = proj(C_CQ, 256)
    cqn = cq * lax.rsqrt(jnp.mean(cq * cq, axis=-1, keepdims=True) + NORM_EPS) * vec(V_GCQ, 256)
    q = _dot(cqn.astype(BF16), wuq_ref[...])
    q = q * lax.rsqrt(_group_sumsq(q, e128_ref) * (1.0 / MLA_DK) + NORM_EPS) * vec(V_MGQ, 512)
    mq_ref[0] = rope(q, cos_m, sa_m, sb_m).astype(BF16)
    ckv = proj(C_CKV, 128)
    ckvn = (ckv * lax.rsqrt(jnp.mean(ckv * ckv, axis=-1, keepdims=True) + NORM_EPS) * vec(V_GCKV, 128)).astype(BF16)
    k = _dot(ckvn, wuk_ref[...]) + tile_lanes(proj(C_KR, 128), 4)
    k = k * lax.rsqrt(_group_sumsq(k, e128_ref) * (1.0 / MLA_DK) + NORM_EPS) * vec(V_MGK, 512)
    k = jnp.where(tile_lanes(lane128 == MLA_DK, 4), 1.0, rope(k, cos_m, sa_m, sb_m))
    mkt_ref[0] = k.T.astype(BF16)
    mv_ref[0] = _dot(ckvn, wuv_ref[...]).astype(BF16)

    u = proj(C_LQ, 512)
    w_lqk = w_ref[:, C_LQ:C_LQ + 512]
    u_prev = _dot(normmod(xp_ref[0], i * tm - 8), w_lqk)[7:8, :]
    u_next = _dot(normmod(xn_ref[0], (i + 1) * tm), w_lqk)[0:1, :]
    row = lax.broadcasted_iota(jnp.int32, (tm, 512), 0)
    tok = i * tm + row
    no_prev = (tok == 0) | (tok == t_lat)
    no_next = (tok == t_lat - 1) | (tok == t_all - 1)
    up = jnp.where(no_prev, 0.0, jnp.where(row == 0, u_prev, pltpu.roll(u, 1, axis=0)))
    dn = jnp.where(no_next, 0.0, jnp.where(row == tm - 1, u_next, pltpu.roll(u, tm - 1, axis=0)))
    conv = vec(V_CW, 512) * up + vec(V_CW + 1, 512) * u + vec(V_CW + 2, 512) * dn
    qk = conv * _sigmoid(conv)
    lq_ref[0] = qk[:, 0:256].astype(BF16)
    lk_ref[0] = (qk[:, 256:512] * (HEAD_DIM ** -0.5)).astype(BF16)
    lv_ref[0] = proj(C_LV, 256).astype(BF16)
    lo_ref[0] = _sigmoid(proj(C_LO, 256)).astype(BF16)
    gt = proj(C_LG, 128) + vec(V_GB, 128)
    log_sig = jnp.minimum(gt, 0.0) - jnp.log1p(jnp.exp(-jnp.abs(gt)))
    lg_ref[0] = jnp.where(lane128 < 2 * N_HEADS, gt, log_sig)

    xq = proj(C_NQ, 256)
    nq_ref[0] = (xq * lax.rsqrt(_group_sumsq(xq, e64_ref) * (1.0 / HEAD_DIM) + NORM_EPS) * vec(V_NGQ, 256)).astype(BF16)
    xk = proj(C_NK, 256)
    nk_ref[0] = (xk * lax.rsqrt(_group_sumsq(xk, e64_ref) * (1.0 / HEAD_DIM) + NORM_EPS) * vec(V_NGK, 256)).astype(BF16)
    nv_ref[0] = proj(C_NV, 256).astype(BF16)


def _inproj(x_all, modv, g1, w1, tab, consts, vecs, wuq, wuk, wuv, t_lat):
    bsz, tt, d = x_all.shape
    tm = _row_tile(tt)
    n_tiles = tt // tm
    last8 = tt // 8 - 1

    def full(a):
        return pl.BlockSpec(a.shape, lambda i, b: (0,) * a.ndim)

    def tok(width):
        return pl.BlockSpec((1, tm, width), lambda i, b: (b, i, 0))

    def tok_t(rows):
        return pl.BlockSpec((1, rows, tm), lambda i, b: (b, 0, i))

    def sds(shape, dt=BF16):
        return jax.ShapeDtypeStruct(shape, dt)

    e32, e64, e128 = consts
    outs = [
        (sds((bsz, tt, 256)), tok(256)), (sds((bsz, 8 * DIFF_KS, tt)), tok_t(8 * DIFF_KS)), (sds((bsz, tt, 256)), tok(256)),
        (sds((bsz, tt, 512)), tok(512)), (sds((bsz, 512, tt)), tok_t(512)), (sds((bsz, tt, 256)), tok(256)),
        (sds((bsz, tt, 256)), tok(256)), (sds((bsz, tt, 256)), tok(256)), (sds((bsz, tt, 256)), tok(256)),
        (sds((bsz, tt, 256)), tok(256)), (sds((bsz, tt, 128), F32), tok(128)),
        (sds((bsz, tt, 256)), tok(256)), (sds((bsz, tt, 256)), tok(256)), (sds((bsz, tt, 256)), tok(256)),
    ]
    return pl.pallas_call(
        functools.partial(_inproj_kernel, t_lat=t_lat, t_all=tt),
        grid=(n_tiles, bsz),
        in_specs=[
            pl.BlockSpec((1, tm, d), lambda i, b: (b, i, 0)),
            pl.BlockSpec((1, 8, d), lambda i, b: (b, jnp.maximum(i * (tm // 8) - 1, 0), 0)),
            pl.BlockSpec((1, 8, d), lambda i, b: (b, jnp.minimum((i + 1) * (tm // 8), last8), 0)),
            pl.BlockSpec((1, 2, 8, d), lambda i, b: (b, 0, 0, 0)),
            full(g1), full(w1),
            pl.BlockSpec((tm, 384), lambda i, b: (i, 0)),
            full(e32), full(e64), full(e128), full(vecs), full(wuq), full(wuk), full(wuv),
        ],
        out_specs=[o[1] for o in outs],
        out_shape=[o[0] for o in outs],
        compiler_params=_cparams(("parallel", "parallel")),
        name="inproj",
    )(x_all, x_all, x_all, modv, g1, w1, tab, e32, e64, e128, vecs, wuq, wuk, wuv)


def _attn_kernel(bound_ref, q_ref, k_ref, v_ref, lam_ref, gout_ref, o_ref, q_scr, m_scr, l_scr, acc_scr, *,
                 n_maps, qs, ks, aug_lane, transposed, lam_init):
    ki = pl.program_id(2)
    n_hm = N_HEADS * n_maps
    tq = q_ref.shape[1]
    tk = v_ref.shape[1]
    lane_head = lax.broadcasted_iota(jnp.int32, (tq, GROUP_W), 1) // HEAD_DIM
    bound = bound_ref[0]

    @pl.when(ki == 0)
    def _init():
        if transposed:
            lane = lax.broadcasted_iota(jnp.int32, (tq, ks), 1)
            for hm in range(n_hm):
                qh = q_ref[0, :, hm * qs:(hm + 1) * qs].astype(F32)
                if ks > qs:
                    qh = jnp.concatenate([qh, jnp.zeros((tq, ks - qs), F32)], axis=1)
                q_scr[hm] = jnp.where(lane == aug_lane, -bound, qh).astype(BF16)
        else:
            for hm in range(n_hm):
                q_scr[hm] = jnp.where(lane_head == hm, q_ref[0], jnp.zeros_like(q_ref[0]))
        m_scr[...] = jnp.full(m_scr.shape, NEG, F32)
        l_scr[...] = jnp.zeros(l_scr.shape, F32)
        acc_scr[...] = jnp.zeros(acc_scr.shape, F32)

    def scores(hm):
        if transposed:
            return _dot(q_scr[hm], k_ref[0, hm * ks:(hm + 1) * ks, :])
        return _dot_nt(q_scr[hm], k_ref[0])

    def lane_partial_sums(p):
        return functools.reduce(lambda a, b: a + b, [p[:, c:c + 128] for c in range(0, tk, 128)])

    def single_pass():
        v = v_ref[0]
        for hm in range(n_hm):
            p = jnp.exp2(scores(hm))
            l_scr[hm] += lane_partial_sums(p)
            acc_scr[hm] += _dot(p.astype(BF16), v)

    def online():
        v = v_ref[0]
        for hm in range(n_hm):
            s = scores(hm)
            m_prev = m_scr[hm]
            m_new = jnp.maximum(m_prev, jnp.max(s, axis=-1, keepdims=True))
            alpha = jnp.exp2(m_prev - m_new)
            p = jnp.exp2(s - m_new)
            l_scr[hm] = alpha * l_scr[hm] + lane_partial_sums(p)
            acc_scr[hm] = alpha * acc_scr[hm] + _dot(p.astype(BF16), v)
            m_scr[hm] = m_new

    if transposed:
        pl.when(bound <= BOUND_LIMIT)(single_pass)
        pl.when(bound > BOUND_LIMIT)(online)
    else:
        online()

    @pl.when(ki == pl.num_programs(2) - 1)
    def _finish():
        out = jnp.zeros((tq, GROUP_W), F32)
        if n_maps == 2:
            lv = lam_ref[...]
            lam = (jnp.exp(jnp.sum(lv[0:1] * lv[1:2], axis=-1, keepdims=True))
                   - jnp.exp(jnp.sum(lv[2:3] * lv[3:4], axis=-1, keepdims=True)) + lam_init)

        def normalised(hm):
            return acc_scr[hm] / jnp.sum(l_scr[hm], axis=-1, keepdims=True)

        for h in range(N_HEADS):
            if n_maps == 2:
                o = normalised(2 * h) - lam * normalised(2 * h + 1)
                ms = jnp.sum(jnp.where(lane_head == h, o * o, 0.0), axis=-1, keepdims=True) * (1.0 / HEAD_DIM)
                o = o * lax.rsqrt(ms + NORM_EPS) * gout_ref[...] * (1.0 - lam_init)
            else:
                o = normalised(h)
            out = jnp.where(lane_head == h, o, out)
        o_ref[0] = out.astype(o_ref.dtype)


def _attention(bound, q, k, v, lam, gout, *, q_off, n_q, tq, k_off, n_k, tk, n_maps, qs, ks, aug_lane, transposed,
               lam_init):
    bsz = q.shape[0]
    n_hm = N_HEADS * n_maps
    if transposed:
        k_spec = pl.BlockSpec((1, k.shape[1], tk), lambda b, i, j: (b, 0, j + k_off))
        q_scr = pltpu.VMEM((n_hm, tq, ks), BF16)
    else:
        k_spec = pl.BlockSpec((1, tk, k.shape[2]), lambda b, i, j: (b, j + k_off, 0))
        q_scr = pltpu.VMEM((n_hm, tq, GROUP_W), BF16)
    return pl.pallas_call(
        functools.partial(_attn_kernel, n_maps=n_maps, qs=qs, ks=ks, aug_lane=aug_lane, transposed=transposed,
                          lam_init=lam_init),
        grid=(bsz, n_q, n_k),
        in_specs=[
            pl.BlockSpec(memory_space=pltpu.SMEM),
            pl.BlockSpec((1, tq, q.shape[2]), lambda b, i, j: (b, i + q_off, 0)),
            k_spec,
            pl.BlockSpec((1, tk, GROUP_W), lambda b, i, j: (b, j + k_off, 0)),
            pl.BlockSpec(lam.shape, lambda b, i, j: (0, 0)),
            pl.BlockSpec(gout.shape, lambda b, i, j: (0, 0)),
        ],
        out_specs=pl.BlockSpec((1, tq, GROUP_W), lambda b, i, j: (b, i, 0)),
        out_shape=jax.ShapeDtypeStruct((bsz, n_q * tq, GROUP_W), BF16),
        scratch_shapes=[q_scr, pltpu.VMEM((n_hm, tq, 1), F32), pltpu.VMEM((n_hm, tq, 128), F32),
                        pltpu.VMEM((n_hm, tq, GROUP_W), F32)],
        compiler_params=_cparams(("parallel", "parallel", "arbitrary")),
        name="attn",
    )(bound, q, k, v, lam, gout)


def _split2(x):
    hi = x.astype(BF16)
    return hi, (x - hi.astype(F32)).astype(BF16)


def _mlstm_kernel(qf_ref, kf_ref, vf_ref, gf_ref, qb_ref, kb_ref, vb_ref, gb_ref, x_ref, tri_ref,
                  of_ref, ob_ref, ct_scr, n_scr, m_scr, *, n_chunk):
    j = pl.program_id(0)
    lc = MLSTM_CHUNK
    bsz = qf_ref.shape[0]

    @pl.when(j == 0)
    def _init():
        ct_scr[...] = jnp.zeros(ct_scr.shape, F32)
        n_scr[...] = jnp.zeros(n_scr.shape, F32)
        m_scr[...] = jnp.zeros(m_scr.shape, F32)

    row = lax.broadcasted_iota(jnp.int32, (lc, GROUP_W), 0)
    lane = lax.broadcasted_iota(jnp.int32, (lc, GROUP_W), 1)
    pos = lane % lc
    head = lane // HEAD_DIM
    half = lax.broadcasted_iota(jnp.int32, (lc, 128), 1) // HEAD_DIM
    eye_t = pos == row
    blockdiag = (lax.broadcasted_iota(jnp.int32, (GROUP_W, GROUP_W), 0) // HEAD_DIM
                 == lax.broadcasted_iota(jnp.int32, (GROUP_W, GROUP_W), 1) // HEAD_DIM)
    ones_bd = jnp.where(blockdiag, 1.0, 0.0).astype(BF16)

    refs = ((qf_ref, kf_ref, vf_ref, gf_ref, of_ref), (qb_ref, kb_ref, vb_ref, gb_ref, ob_ref))
    causal_by_dir = (pos <= row, pos >= row)
    chains = [(dr, bi) for dr in range(2) for bi in range(bsz)]
    state, gate = {}, {}
    for dr, bi in chains:
        state[dr, bi] = (ct_scr[dr, bi], n_scr[dr, bi], m_scr[dr, bi])
        g_hi, g_lo = _split2(refs[dr][3][bi])
        cum = _dot(tri_ref[dr], g_hi) + _dot(tri_ref[dr], g_lo)
        c_hi, c_lo = _split2(cum)
        to_ig, to_lf = x_ref[dr, :, 0:GROUP_W], x_ref[dr, :, GROUP_W:2 * GROUP_W]
        gate[dr, bi] = (_dot(g_hi, to_ig) + _dot(g_lo, to_ig), _dot(c_hi, to_lf) + _dot(c_lo, to_lf))

    for jj in range(n_chunk):
        for dr, bi in chains:
            q_ref, k_ref, v_ref, _, o_ref = refs[dr]
            causal = causal_by_dir[dr]
            ct, n_s, m_s = state[dr, bi]
            ig_all, b_all = gate[dr, bi]
            c = jj if dr == 0 else n_chunk - 1 - jj
            sl = slice(c * lc, (c + 1) * lc)
            q, k, v = q_ref[bi, sl, :], k_ref[bi, sl, :], v_ref[bi, sl, :]
            ig, b = ig_all[sl], b_all[sl]
            g = b[lc - 1:lc] if dr == 0 else b[0:1]
            a = g - b + ig
            m_loc = jnp.max(a, axis=0, keepdims=True)

            r_row = jnp.sum(jnp.where(eye_t, b - ig, 0.0), axis=0, keepdims=True)
            dlog = jnp.where(causal, b - r_row, NEG)
            inter = b + m_s
            cols = []
            for ci in range(GROUP_W // 128):
                col = slice(ci * 128, (ci + 1) * 128)
                m_a = jnp.max(jnp.where(half == 0, dlog[:, col], NEG), axis=1, keepdims=True)
                m_b = jnp.max(jnp.where(half == 1, dlog[:, col], NEG), axis=1, keepdims=True)
                cols.append(jnp.maximum(inter[:, col], jnp.where(half == 0, m_a, m_b)))
            m_t = jnp.concatenate(cols, axis=1)
            k_bd = jnp.where(blockdiag, jnp.concatenate([k] * N_HEADS, axis=0), jnp.zeros((), BF16))
            v_bd = jnp.where(blockdiag, jnp.concatenate([v] * N_HEADS, axis=0), jnp.zeros((), BF16))
            s = _dot_nt(q, k_bd) * jnp.exp(dlog - m_t)
            e = jnp.exp(inter - m_t)
            s_hi = s.astype(BF16)
            s_lo = (s - s_hi.astype(F32)).astype(BF16)
            num = _dot(s_hi, v_bd) + e * _dot(q, ct.astype(BF16))
            den = _dot(s_hi, ones_bd) + _dot(s_lo, ones_bd) + e * _dot_hilo(q.astype(F32) * n_s, ones_bd)
            o_ref[bi, sl, :] = num / jnp.maximum(jnp.abs(den), jnp.exp(-m_t))

            m_new = jnp.maximum(g + m_s, m_loc)
            a_old = jnp.exp(g + m_s - m_new)
            kw = k.astype(F32) * jnp.exp(a - m_new)
            ct = a_old * ct + jnp.where(blockdiag, _dot_tn(kw.astype(BF16), v), 0.0)
            n_s = a_old * n_s + jnp.sum(kw, axis=0, keepdims=True)
            state[dr, bi] = (ct, n_s, m_new)
    for dr, bi in chains:
        ct_scr[dr, bi], n_scr[dr, bi], m_scr[dr, bi] = state[dr, bi]


def _mlstm(lq, lk, lv, lg, expand, tri, n_lat_tok):
    bsz, tt, _ = lq.shape
    tb = ROW_TILE
    nb = tt // tb
    nbl = n_lat_tok // tb
    nbc = nb - nbl

    def fwd(j):
        return jnp.where(j < nbc, j + nbl, j - nbc)

    def bwd(j):
        return jnp.where(j < nbc, nbl + nbc - 1 - j, nb - 1 - j)

    def tok(width, order):
        return pl.BlockSpec((bsz, tb, width), lambda j: (0, order(j), 0))

    def full(a):
        return pl.BlockSpec(a.shape, lambda j: (0,) * a.ndim)

    out = jax.ShapeDtypeStruct((bsz, tt, GROUP_W), F32)
    return pl.pallas_call(
        functools.partial(_mlstm_kernel, n_chunk=tb // MLSTM_CHUNK),
        grid=(nb,),
        in_specs=[tok(GROUP_W, fwd), tok(GROUP_W, fwd), tok(GROUP_W, fwd), tok(128, fwd),
                  tok(GROUP_W, bwd), tok(GROUP_W, bwd), tok(GROUP_W, bwd), tok(128, bwd),
                  full(expand), full(tri)],
        out_specs=[tok(GROUP_W, fwd), tok(GROUP_W, bwd)],
        out_shape=[out, out],
        scratch_shapes=[pltpu.VMEM((2, bsz, GROUP_W, GROUP_W), F32), pltpu.VMEM((2, bsz, 1, GROUP_W), F32),
                        pltpu.VMEM((2, bsz, 1, GROUP_W), F32)],
        compiler_params=_cparams(("arbitrary",)),
        name="mlstm",
    )(lq, lk, lv, lg, lq, lk, lv, lg, expand, tri)


def _nbr_kernel(q_ref, kp_ref, kc_ref, kn_ref, vp_ref, vc_ref, vn_ref, kx_ref, vx_ref, bias_ref, o_ref,
                kw_scr, vw_scr, *, n_rows):
    i = pl.program_id(1)
    blk = NA_ROWS * GRID_W
    kw_scr[0:blk] = kp_ref[0]
    kw_scr[blk:2 * blk] = kc_ref[0]
    kw_scr[2 * blk:3 * blk] = kn_ref[0]
    vw_scr[0:blk] = vp_ref[0]
    vw_scr[blk:2 * blk] = vc_ref[0]
    vw_scr[2 * blk:3 * blk] = vn_ref[0]
    kx = kx_ref[0]
    vx = vx_ref[0]
    lane_head = lax.broadcasted_iota(jnp.int32, (GRID_W, GROUP_W), 1) // HEAD_DIM
    blockdiag = (lax.broadcasted_iota(jnp.int32, (GROUP_W, GROUP_W), 0) // GRID_W
                 == lax.broadcasted_iota(jnp.int32, (GROUP_W, GROUP_W), 1) // HEAD_DIM)

    for rr in range(NA_ROWS):
        r = i * NA_ROWS + rr
        rs = jnp.clip(r - NA_ROWS // 2, 0, n_rows - NA_ROWS)
        off = pl.multiple_of((rs - (i - 1) * NA_ROWS) * GRID_W, GRID_W)
        delta = r - rs
        q = q_ref[0, rr * GRID_W:(rr + 1) * GRID_W, :]
        kw = kw_scr[pl.ds(off, blk), :]
        vw = vw_scr[pl.ds(off, blk), :]
        q_bd = jnp.where(blockdiag, jnp.concatenate([q] * N_HEADS, axis=0), jnp.zeros((), BF16))
        sw = _dot_nt(q_bd, kw) + bias_ref[delta]
        sx = _dot_nt(q_bd, kx)
        m = jnp.maximum(jnp.max(sw, axis=-1, keepdims=True), jnp.max(sx, axis=-1, keepdims=True))
        pw = jnp.exp2(sw - m)
        px = jnp.exp2(sx - m)
        l = jnp.sum(pw, axis=-1, keepdims=True) + jnp.sum(px, axis=-1, keepdims=True)
        o = (_dot(pw.astype(BF16), vw) + _dot(px.astype(BF16), vx)) / l
        out = jnp.zeros((GRID_W, GROUP_W), F32)
        for h in range(N_HEADS):
            out = jnp.where(lane_head == h, o[h * GRID_W:(h + 1) * GRID_W], out)
        o_ref[0, rr * GRID_W:(rr + 1) * GRID_W, :] = out.astype(o_ref.dtype)


def _neighbourhood(nq, nk, nv, bias, n_lat_tok, n_ctx_tok):
    bsz = nq.shape[0]
    n_rows = n_lat_tok // GRID_W
    blk = NA_ROWS * GRID_W
    nblk = n_lat_tok // blk
    ctx_blk = n_lat_tok // n_ctx_tok

    def win(shift):
        return pl.BlockSpec((1, blk, GROUP_W), lambda b, i: (b, jnp.clip(i + shift, 0, nblk - 1), 0))

    ctx_spec = pl.BlockSpec((1, n_ctx_tok, GROUP_W), lambda b, i: (b, ctx_blk, 0))
    return pl.pallas_call(
        functools.partial(_nbr_kernel, n_rows=n_rows),
        grid=(bsz, nblk),
        in_specs=[win(0), win(-1), win(0), win(1), win(-1), win(0), win(1), ctx_spec, ctx_spec,
                  pl.BlockSpec(bias.shape, lambda b, i: (0, 0, 0))],
        out_specs=pl.BlockSpec((1, blk, GROUP_W), lambda b, i: (b, i, 0)),
        out_shape=jax.ShapeDtypeStruct((bsz, n_lat_tok, GROUP_W), BF16),
        scratch_shapes=[pltpu.VMEM((3 * blk, GROUP_W), BF16), pltpu.VMEM((3 * blk, GROUP_W), BF16)],
        compiler_params=_cparams(("parallel", "parallel")),
        name="nbr",
    )(nq, nk, nk, nk, nv, nv, nv, nk, nv, bias)


def _outproj_kernel(x_ref, mod_ref, yd_ref, ym_ref, hf_ref, hb_ref, lo_ref, yn_ref, gml_ref, g2_ref, w_ref, e64_ref,
                    x1_ref, h2_ref, *, t_lat):
    tm, d = x_ref.shape[1], x_ref.shape[2]
    is_ctx = pl.program_id(0) * tm + lax.broadcasted_iota(jnp.int32, (tm, d), 0) >= t_lat

    def mod(r):
        return jnp.where(is_ctx, mod_ref[0, 1, r:r + 1, :], mod_ref[0, 0, r:r + 1, :])

    hs = hf_ref[0] + hb_ref[0]
    ms = _group_sumsq(hs, e64_ref) * (1.0 / HEAD_DIM)
    yl = (hs * lax.rsqrt(ms + NORM_EPS) * gml_ref[...] * lo_ref[0].astype(F32)).astype(BF16)
    o = (_dot(yd_ref[0], w_ref[0:256, :]) + _dot(ym_ref[0], w_ref[256:512, :])
         + _dot(yl, w_ref[512:768, :]) + _dot(yn_ref[0], w_ref[768:1024, :]))
    x1 = x_ref[0] + mod(2) * o
    x1_ref[0] = x1
    ms2 = jnp.mean(x1 * x1, axis=-1, keepdims=True)
    h2 = (x1 * lax.rsqrt(ms2 + NORM_EPS)) * g2_ref[...] * mod(4) + mod(3)
    h2_ref[0] = h2.astype(BF16)


def _outproj(x_all, modv, yd, ym, hf, hb, lo, yn, gml, g2, w_out, e64, t_lat, rows):
    bsz, _, d = x_all.shape
    tm = _row_tile(rows)
    n_tiles = rows // tm

    def tok(width):
        return pl.BlockSpec((1, tm, width), lambda i, b: (b, i, 0))

    def full(a):
        return pl.BlockSpec(a.shape, lambda i, b: (0,) * a.ndim)

    return pl.pallas_call(
        functools.partial(_outproj_kernel, t_lat=t_lat),
        grid=(n_tiles, bsz),
        in_specs=[tok(d), pl.BlockSpec((1, 2, 8, d), lambda i, b: (b, 0, 0, 0)),
                  tok(GROUP_W), tok(GROUP_W), tok(GROUP_W), tok(GROUP_W),
                  tok(GROUP_W), tok(GROUP_W), full(gml), full(g2), full(w_out), full(e64)],
        out_specs=[tok(d), tok(d)],
        out_shape=[jax.ShapeDtypeStruct((bsz, n_tiles * tm, d), F32),
                   jax.ShapeDtypeStruct((bsz, n_tiles * tm, d), BF16)],
        compiler_params=_cparams(("parallel", "parallel")),
        name="outproj",
    )(x_all, modv, yd, ym, hf, hb, lo, yn, gml, g2, w_out, e64)


def _ffn_kernel(x1_ref, h_ref, hp_ref, hn_ref, mod_ref, wup_ref, wc_ref, wdn_ref, o_ref, act_scr, *, t_lat, t_all):
    i = pl.program_id(0)
    tm, d = h_ref.shape[1], h_ref.shape[2]
    lhs = jnp.concatenate([hp_ref[0], h_ref[0], hn_ref[0]], axis=0)
    row = lax.broadcasted_iota(jnp.int32, (tm, FFN_CHUNK), 0)
    tok = i * tm + row
    first, last = row == 0, row == tm - 1
    no_prev = (tok == 0) | (tok == t_lat)
    no_next = (tok == t_lat - 1) | (tok == t_all - 1)

    def conv_up(c0):
        uu = _dot(lhs, wup_ref[:, c0:c0 + FFN_CHUNK])
        u = uu[FFN_HALO:FFN_HALO + tm]
        up = jnp.where(first, uu[FFN_HALO - 1:FFN_HALO], pltpu.roll(u, 1, axis=0))
        dn = jnp.where(last, uu[FFN_HALO + tm:FFN_HALO + tm + 1], pltpu.roll(u, tm - 1, axis=0))
        return (wc_ref[0:1, c0:c0 + FFN_CHUNK] * jnp.where(no_prev, 0.0, up) + wc_ref[1:2, c0:c0 + FFN_CHUNK] * u
                + wc_ref[2:3, c0:c0 + FFN_CHUNK] * jnp.where(no_next, 0.0, dn))

    for c in range(FFN_HIDDEN // FFN_CHUNK):
        a = conv_up(c * FFN_CHUNK)
        g = conv_up(FFN_HIDDEN + c * FFN_CHUNK)
        act_scr[:, c * FFN_CHUNK:(c + 1) * FFN_CHUNK] = (g * _sigmoid(g) * a).astype(BF16)
    down = _dot(act_scr[...], wdn_ref[...])
    is_ctx = i * tm + lax.broadcasted_iota(jnp.int32, (tm, d), 0) >= t_lat
    gate = jnp.where(is_ctx, mod_ref[0, 1, 5:6, :], mod_ref[0, 0, 5:6, :])
    o_ref[0] = x1_ref[0] + gate * down


def _ffn(x1, h2, modv, w_up, w_conv, w_down, t_lat, tm):
    bsz, rows, d = x1.shape
    n_tiles = rows // tm
    per = tm // FFN_HALO
    last = rows // FFN_HALO - 1

    def tok():
        return pl.BlockSpec((1, tm, d), lambda i, b: (b, i, 0))

    def resident(a):
        return pl.BlockSpec(a.shape, lambda i, b: (0,) * a.ndim, pipeline_mode=pl.Buffered(1))

    return pl.pallas_call(
        functools.partial(_ffn_kernel, t_lat=t_lat, t_all=rows),
        grid=(n_tiles, bsz),
        in_specs=[tok(), tok(),
                  pl.BlockSpec((1, FFN_HALO, d), lambda i, b: (b, jnp.maximum(i * per - 1, 0), 0)),
                  pl.BlockSpec((1, FFN_HALO, d), lambda i, b: (b, jnp.minimum((i + 1) * per, last), 0)),
                  pl.BlockSpec((1, 2, 8, d), lambda i, b: (b, 0, 0, 0)),
                  resident(w_up), resident(w_conv), resident(w_down)],
        out_specs=tok(),
        out_shape=jax.ShapeDtypeStruct((bsz, rows, d), F32),
        scratch_shapes=[pltpu.VMEM((tm, FFN_HIDDEN), BF16)],
        compiler_params=_cparams(("parallel", "parallel")),
        name="ffn",
    )(x1, h2, h2, h2, modv, w_up, w_conv, w_down)


def _block_ones(width, group):
    idx = np.arange(width) // group
    return jnp.asarray(idx[:, None] == idx[None, :], dtype=BF16)


def _rope_table(t_len, n_ctx):
    nf = MLA_ROPE // 4
    t = jnp.arange(t_len, dtype=jnp.int32)
    row = (t // GRID_W).astype(F32)
    col = (t % GRID_W).astype(F32)
    inv = ROPE_BASE ** (-jnp.arange(nf, dtype=F32) / nf)
    ar, ac = row[:, None] * inv, col[:, None] * inv
    cr, sr, cc, sc = jnp.cos(ar), jnp.sin(ar), jnp.cos(ac), jnp.sin(ac)
    z = jnp.zeros_like(sr)
    cos = jnp.tile(jnp.concatenate([cr, cr, cc, cc], -1), (1, 4))
    sa = jnp.tile(jnp.concatenate([-sr, z, -sc, z], -1), (1, 4))
    sb = jnp.tile(jnp.concatenate([z, sr, z, sc], -1), (1, 4))
    lat = jnp.concatenate([cos, sa, sb], -1)
    ctx = jnp.concatenate([jnp.ones((n_ctx, 128), F32), jnp.zeros((n_ctx, 256), F32)], -1)
    return jnp.concatenate([lat, ctx], 0)


def _pack_w_in(w):
    z = lambda n: jnp.zeros(w.shape[:-1] + (n,), w.dtype)
    return jnp.concatenate([w[..., 0:1152], z(64), w[..., 1152:1184], z(32), w[..., 1184:2208],
                            w[..., 2208:2224], z(112), w[..., 2224:2992]], axis=-1).astype(BF16)


def _pad_heads(v, real, padded):
    lead = v.shape[:-1]
    v = v.reshape(lead + (N_HEADS, real))
    v = jnp.pad(v, [(0, 0)] * len(lead) + [(0, 0), (0, padded - real)])
    return v.reshape(lead + (N_HEADS * padded,))


def _pack_vectors(dgq, dgk, gcq, gckv, mgq, mgk, ngq, ngk, b_i, b_f, w_conv):
    n_layers = dgq.shape[0]

    def row(v):
        return jnp.pad(v.astype(F32), ((0, 0), (0, 512 - v.shape[1])))[:, None]

    rows = [
        row(jnp.tile(dgq, (1, 8)) * (DIFF_DK ** -0.5 * LOG2E)), row(jnp.tile(dgk, (1, 8))),
        row(gcq), row(gckv),
        row(_pad_heads(jnp.tile(mgq, (1, N_HEADS)), MLA_DK, MLA_PAD) * (MLA_DK ** -0.5 * LOG2E)),
        row(_pad_heads(jnp.tile(mgk, (1, N_HEADS)), MLA_DK, MLA_PAD)),
        row(jnp.tile(ngq, (1, N_HEADS)) * (HEAD_DIM ** -0.5 * LOG2E)), row(jnp.tile(ngk, (1, N_HEADS))),
        row(jnp.concatenate([b_i.reshape(n_layers, -1), b_f.reshape(n_layers, -1)], axis=1)),
        w_conv.astype(F32),
    ]
    rows = jnp.concatenate(rows, axis=1)
    return jnp.pad(rows, ((0, 0), (0, 16 - rows.shape[1]), (0, 0)))


def _score_bound(g_q, g_k, n_dims):
    scale = n_dims ** -0.5 * LOG2E
    return (n_dims * scale * jnp.max(jnp.abs(g_q), axis=-1) * jnp.max(jnp.abs(g_k), axis=-1) * 1.02 + 0.01).astype(F32)


def _nbr_bias(rpb):
    j = np.arange(NA_ROWS)
    delta = np.arange(NA_ROWS)
    roff = j[None, :] - delta[:, None] + (NA_ROWS - 1)
    cidx = np.arange(GRID_W)
    coff = np.clip(cidx[None, :] - cidx[:, None], 1 - NA_COLS, NA_COLS - 1) + (NA_COLS - 1)
    cs = np.clip(cidx - NA_COLS // 2, 0, GRID_W - NA_COLS)
    col_ok = (cidx[None, :] >= cs[:, None]) & (cidx[None, :] < cs[:, None] + NA_COLS)
    sel_r = jnp.asarray(roff[:, :, None] == np.arange(2 * NA_ROWS - 1), F32)
    sel_c = jnp.asarray(coff[:, :, None] == np.arange(2 * NA_COLS - 1), F32)
    bias = jnp.einsum('lhrc,djr,qkc->ldhqjk', rpb.astype(F32) * LOG2E, sel_r, sel_c, precision=HIGHEST)
    bias = bias + jnp.asarray(np.where(col_ok, 0.0, NEG), F32)[None, None, None, :, None, :]
    return bias.reshape(rpb.shape[0], NA_ROWS, N_HEADS * GRID_W, NA_ROWS * GRID_W)


def _mlstm_consts():
    expand = np.zeros((2, 128, 2 * GROUP_W), np.float32)
    for dr in range(2):
        for h in range(N_HEADS):
            expand[dr, dr * N_HEADS + h, h * HEAD_DIM:(h + 1) * HEAD_DIM] = 1.0
            expand[dr, 2 * N_HEADS + dr * N_HEADS + h, GROUP_W + h * HEAD_DIM:GROUP_W + (h + 1) * HEAD_DIM] = 1.0
    lower = np.kron(np.eye(ROW_TILE // MLSTM_CHUNK), np.tril(np.ones((MLSTM_CHUNK, MLSTM_CHUNK))))
    return jnp.asarray(expand, BF16), jnp.asarray(np.stack([lower, lower.T]), BF16)


def _row_tile(rows):
    return next(t for t in (1024, 768, 512, 256) if rows % t == 0)


def _kv_tile(total):
    best = 128
    for t in range(128, 1025, 128):
        if total % t == 0:
            best = t
    return best


def kernel(x, c, ctx, c_ctx, w_mod, b_mod, g_norm1, g_norm2, w_in, w_out, diff_g_q, diff_g_k, diff_lam, diff_g_out, mla_g_cq, mla_g_ckv, mla_w_uq, mla_w_ukv, mla_g_q, mla_g_k, mlstm_w_conv, mlstm_b_i, mlstm_b_f, mlstm_g_out, na_g_q, na_g_k, na_rpb, ffn_w_up, ffn_w_conv, ffn_w_down):
    bsz, t_lat, d = x.shape
    n_ctx = ctx.shape[1]
    depth = w_in.shape[0]
    tt = t_lat + n_ctx
    tm = ROW_TILE
    assert d == D_MODEL and t_lat % (NA_ROWS * GRID_W) == 0 and n_ctx % tm == 0 and bsz < 8
    n_lat, n_tiles = t_lat // tm, tt // tm

    c_rows = jnp.zeros((8, d), F32).at[:bsz].set(c).at[bsz].set(c_ctx)
    mod_all = _modulation(c_rows, w_mod, b_mod).reshape(depth, 8, 6, d)
    mod_lat = jnp.pad(mod_all[:, :bsz], ((0, 0), (0, 0), (0, 2), (0, 0)))
    mod_ctx = jnp.broadcast_to(jnp.pad(mod_all[:, bsz], ((0, 0), (0, 2), (0, 0)))[:, None], mod_lat.shape)
    modv_all = jnp.stack([mod_lat, mod_ctx], axis=2) + jnp.asarray([0, 1, 0, 0, 1, 0, 0, 0], F32)[:, None]

    tab = _rope_table(t_lat, n_ctx)
    consts = (_block_ones(256, DIFF_DK), _block_ones(256, HEAD_DIM), _block_ones(256, MLA_PAD))
    expand, tri = _mlstm_consts()
    tk_lat = _kv_tile(tt)

    w1_all = _pack_w_in(w_in)
    wukv = mla_w_ukv.reshape(depth, MLA_KV_LORA, N_HEADS, MLA_NOPE + HEAD_DIM)
    wuk_all = jnp.pad(wukv[..., :MLA_NOPE], ((0, 0), (0, 0), (0, 0), (0, MLA_PAD - MLA_NOPE))
                      ).reshape(depth, MLA_KV_LORA, -1).astype(BF16)
    wuv_all = wukv[..., MLA_NOPE:].reshape(depth, MLA_KV_LORA, -1).astype(BF16)
    wuq_all = _pad_heads(mla_w_uq, MLA_DK, MLA_PAD).astype(BF16)
    vecs_all = _pack_vectors(diff_g_q, diff_g_k, mla_g_cq, mla_g_ckv, mla_g_q, mla_g_k, na_g_q, na_g_k,
                             mlstm_b_i, mlstm_b_f, mlstm_w_conv)
    bound_d = _score_bound(diff_g_q, diff_g_k, DIFF_DK)
    bound_m = _score_bound(mla_g_q, mla_g_k, MLA_DK)
    bias_all = _nbr_bias(na_rpb)
    gout_all = jnp.tile(diff_g_out.astype(F32), (1, N_HEADS))
    gml_all = jnp.tile(mlstm_g_out.astype(F32), (1, N_HEADS))
    lam_all = diff_lam.astype(F32)
    w_out_all, w_up_all, w_down_all = w_out.astype(BF16), ffn_w_up.astype(BF16), ffn_w_down.astype(BF16)
    w_conv_all = ffn_w_conv.astype(F32)

    x_all = jnp.concatenate([x, ctx], axis=1)
    for l in range(depth):
        need_ctx = l < depth - 1
        lam_init = 0.8 - 0.6 * math.exp(-0.3 * l)
        modv = modv_all[l]
        (dq, dkt, dv, mq, mkt, mv, lq, lk, lv, lo, lg, nq, nk, nv) = _inproj(
            x_all, modv, g_norm1[l][None], w1_all[l], tab, consts, vecs_all[l], wuq_all[l], wuk_all[l], wuv_all[l],
            t_lat)

        lam, gout = lam_all[l], gout_all[l][None]
        bd, bm = bound_d[l:l + 1], bound_m[l:l + 1]
        lat_q = dict(q_off=0, n_q=t_lat // ATTN_TQ, tq=ATTN_TQ, k_off=0, n_k=tt // tk_lat, tk=tk_lat)
        ctx_q = dict(q_off=n_lat, n_q=n_ctx // tm, tq=tm, k_off=t_lat // n_ctx, n_k=1, tk=n_ctx)
        diff_cfg = dict(n_maps=2, qs=DIFF_DK, ks=DIFF_KS, aug_lane=DIFF_DK, transposed=True, lam_init=lam_init)
        mla_cfg = dict(n_maps=1, qs=MLA_PAD, ks=MLA_PAD, aug_lane=MLA_DK, transposed=True, lam_init=0.0)
        yd = _attention(bd, dq, dkt, dv, lam, gout, **lat_q, **diff_cfg)
        ym = _attention(bm, mq, mkt, mv, lam, gout, **lat_q, **mla_cfg)
        yn = _neighbourhood(nq, nk, nv, bias_all[l], t_lat, n_ctx)
        hf, hb = _mlstm(lq, lk, lv, lg, expand, tri, t_lat)
        if need_ctx:
            na_cfg = dict(n_maps=1, qs=HEAD_DIM, ks=HEAD_DIM, aug_lane=0, transposed=False, lam_init=0.0)
            yd = jnp.concatenate([yd, _attention(bd, dq, dkt, dv, lam, gout, **ctx_q, **diff_cfg)], axis=1)
            ym = jnp.concatenate([ym, _attention(bm, mq, mkt, mv, lam, gout, **ctx_q, **mla_cfg)], axis=1)
            yn = jnp.concatenate([yn, _attention(bd, nq, nk, nv, lam, gout, **ctx_q, **na_cfg)], axis=1)
        rows = tt if need_ctx else t_lat
        x1, h2 = _outproj(x_all, modv, yd, ym, hf, hb, lo, yn, gml_all[l][None], g_norm2[l][None], w_out_all[l],
                          consts[1], t_lat, rows)
        x_all = _ffn(x1, h2, modv, w_up_all[l], w_conv_all[l], w_down_all[l], t_lat, _row_tile(rows))
    return x_all[:, :t_lat]
```
